```python
import jax
import jax.numpy as jnp
from jax import lax
import numpy as np


D_MODEL = 1024
BATCH = 8
SEQ = 2048
DEPTH = 4

GRID_W = 64
CTX_LEN = 256
N_MIXERS = 3
N_HEADS = 16
N_KV_HEADS = 4
HEAD_DIM = D_MODEL // N_HEADS
Q_GROUP = N_HEADS // N_KV_HEADS
ROPE_FREQS = HEAD_DIM // 4
ROPE_THETA = 10000.0
Q_BLOCK = 128
CONV_WIDTH = 31
D_RNN = D_MODEL
N_LRU_BLOCKS = 4
LRU_BLOCK = D_RNN // N_LRU_BLOCKS
LRU_CONV_WIDTH = 4
LRU_C = 8.0
D_FF = 4 * D_MODEL
EPS = 1e-6
N_A = (DEPTH + N_MIXERS - 1) // N_MIXERS
N_B = (DEPTH + N_MIXERS - 2) // N_MIXERS
N_C = (DEPTH + N_MIXERS - 3) // N_MIXERS

kernel_name = 'hybrid_interleaved_dit_block'

F32 = jnp.float32


def _rms_f32(x, g):
    xf = x.astype(F32)
    return xf * lax.rsqrt(jnp.mean(xf * xf, axis=-1, keepdims=True) + EPS) * g.astype(F32)


def rms_norm(x, g):
    return _rms_f32(x, g).astype(x.dtype)


def layer_norm(x, g, b):
    xf = x.astype(F32)
    xc = xf - jnp.mean(xf, axis=-1, keepdims=True)
    var = jnp.mean(xc * xc, axis=-1, keepdims=True)
    return (xc * lax.rsqrt(var + EPS) * g.astype(F32) + b.astype(F32)).astype(x.dtype)


def ada_mod(cond, w, b):
    m = jax.nn.silu(cond) @ w + b
    return jnp.split(m[:, None, :], 6, axis=-1)


def modulate(h, shift, scale):
    return h * (1.0 + scale) + shift


def depthwise_conv(x, w, b, pad):
    y = lax.conv_general_dilated(x, w[:, None, :].astype(x.dtype), window_strides=(1,), padding=[pad],
                                 dimension_numbers=('NWC', 'WIO', 'NWC'), feature_group_count=x.shape[-1])
    return y + b


def axial_rope_tables(n_tokens):
    rows = n_tokens // GRID_W
    row = jnp.repeat(jnp.arange(rows, dtype=jnp.int32), GRID_W)
    col = jnp.tile(jnp.arange(GRID_W, dtype=jnp.int32), rows)
    pos = jnp.stack([row, col], axis=-1).astype(F32)
    inv = ROPE_THETA ** (-jnp.arange(ROPE_FREQS, dtype=F32) / ROPE_FREQS)
    ang = pos[:, :, None] * inv
    return jnp.cos(ang), jnp.sin(ang)


def apply_axial_rope(x, cos, sin):
    b, n, h, _ = x.shape
    xr = x.reshape(b, n, h, 2, 2, ROPE_FREQS)
    x1, x2 = xr[..., 0, :], xr[..., 1, :]
    cs, sn = cos[None, :, None], sin[None, :, None]
    out = jnp.stack([x1 * cs - x2 * sn, x1 * sn + x2 * cs], axis=-2)
    return out.reshape(b, n, h, HEAD_DIM)


def gqa_attend(q, k, v):
    b, nq = q.shape[:2]
    qg = q.reshape(b, nq, N_KV_HEADS, Q_GROUP, HEAD_DIM)
    s = jnp.einsum('bqkgd,btkd->bkgqt', qg, k) * (HEAD_DIM ** -0.5)
    p = jax.nn.softmax(s, axis=-1)
    o = jnp.einsum('bkgqt,btkd->bqkgd', p, v)
    return o.reshape(b, nq, N_HEADS * HEAD_DIM)


def attention_mixer(h_lat, h_ctx, w_qkv, q_gain, k_gain, w_o, need_ctx):
    bsz, n_lat, _ = h_lat.shape
    hq = N_HEADS * HEAD_DIM
    hkv = N_KV_HEADS * HEAD_DIM

    def heads_q(q):
        return _rms_f32(q.reshape(q.shape[0], q.shape[1], N_HEADS, HEAD_DIM), q_gain)

    def heads_kv(kv):
        k, v = kv[..., :hkv], kv[..., hkv:]
        k = _rms_f32(k.reshape(k.shape[0], k.shape[1], N_KV_HEADS, HEAD_DIM), k_gain)
        v = v.reshape(v.shape[0], v.shape[1], N_KV_HEADS, HEAD_DIM).astype(F32)
        return k, v

    qkv_l = h_lat @ w_qkv
    q_l = heads_q(qkv_l[..., :hq])
    k_l, v_l = heads_kv(qkv_l[..., hq:])
    k_c, v_c = heads_kv(h_ctx @ w_qkv[:, hq:])
    cos, sin = axial_rope_tables(n_lat)
    q_l = apply_axial_rope(q_l, cos, sin)
    k_l = apply_axial_rope(k_l, cos, sin)
    k_all = jnp.concatenate([k_c, k_l], axis=1)
    v_all = jnp.concatenate([v_c, v_l], axis=1)
    n_blk = n_lat // Q_BLOCK
    q_blk = q_l.reshape(bsz, n_blk, Q_BLOCK, N_HEADS, HEAD_DIM).transpose(1, 0, 2, 3, 4)
    o_blk = lax.map(lambda qb: gqa_attend(qb, k_all, v_all), q_blk)
    o_l = o_blk.transpose(1, 0, 2, 3).reshape(bsz, n_lat, hq).astype(h_lat.dtype)
    out_l = o_l @ w_o
    if not need_ctx:
        return out_l, None
    q_c = heads_q(h_ctx @ w_qkv[:, :hq])
    o_c = gqa_attend(q_c, k_c, v_c).astype(h_ctx.dtype)
    return out_l, o_c @ w_o


def conformer_mixer(h_lat, h_ctx, w_in, b_in, w_dw, b_dw, n_g, n_b, w_out, b_out, need_ctx):
    half = CONV_WIDTH // 2

    def conv_module(h):
        u = h @ w_in + b_in
        a, g = u[..., :D_MODEL], u[..., D_MODEL:]
        u = a * jax.nn.sigmoid(g)
        u = depthwise_conv(u, w_dw, b_dw, (half, half))
        u = jax.nn.silu(layer_norm(u, n_g, n_b))
        return u @ w_out + b_out

    out_l = conv_module(h_lat)
    if not need_ctx:
        return out_l, None
    return out_l, conv_module(h_ctx)


def _linear_combine(e1, e2):
    a1, b1 = e1
    a2, b2 = e2
    return a1 * a2, a2 * b1 + b2


def rglru_scan(u, gate_w, gate_b, lam, h0):
    bsz, n, _ = u.shape
    ub = u.reshape(bsz, n, N_LRU_BLOCKS, LRU_BLOCK)
    gates = jnp.einsum('blnd,gnde->gblne', ub, gate_w).reshape(2, bsz, n, D_RNN) + gate_b[:, None, None, :]
    gates = jax.nn.sigmoid(gates.astype(F32))
    r, i = gates[0], gates[1]
    log_a = -LRU_C * r * jax.nn.softplus(-lam.astype(F32))
    a = jnp.exp(log_a)
    mult = jnp.sqrt(-jnp.expm1(2.0 * log_a))
    if h0 is None:
        mult = mult.at[:, 0].set(1.0)
    b = mult * i * u.astype(F32)
    a_cum, b_cum = lax.associative_scan(_linear_combine, (a, b), axis=1)
    if h0 is None:
        return b_cum
    return a_cum * h0[:, None, :] + b_cum


def rglru_direction(x_c, x_l, conv_w, conv_b, gate_w, gate_b, lam):
    pad = (LRU_CONV_WIDTH - 1, 0)
    h_c = rglru_scan(depthwise_conv(x_c, conv_w, conv_b, pad), gate_w, gate_b, lam, None)
    h_l = rglru_scan(depthwise_conv(x_l, conv_w, conv_b, pad), gate_w, gate_b, lam, h_c[:, -1])
    return h_l, h_c


def recurrent_mixer(h_lat, h_ctx, w_in, conv_w, conv_b, gate_w, gate_b, lam, w_out, need_ctx):
    gx_l = h_lat @ w_in
    g_l = jax.nn.gelu(gx_l[..., :D_RNN].astype(F32))
    x_l = gx_l[..., D_RNN:]
    if need_ctx:
        gx_c = h_ctx @ w_in
        g_c = jax.nn.gelu(gx_c[..., :D_RNN].astype(F32))
        x_c = gx_c[..., D_RNN:]
    else:
        x_c = h_ctx @ w_in[:, D_RNN:]
    hf_l, hf_c = rglru_direction(x_c, x_l, conv_w[0], conv_b[0], gate_w[0], gate_b[0], lam[0])
    hb_l, hb_c = rglru_direction(x_c[:, ::-1], x_l[:, ::-1], conv_w[1], conv_b[1], gate_w[1], gate_b[1], lam[1])
    y_l = ((hf_l + hb_l[:, ::-1]) * g_l).astype(h_lat.dtype)
    out_l = y_l @ w_out
    if not need_ctx:
        return out_l, None
    y_c = ((hf_c + hb_c[:, ::-1]) * g_c).astype(h_ctx.dtype)
    return out_l, y_c @ w_out


def sq_relu_mlp(h, w1, w2):
    return jnp.square(jax.nn.relu(h @ w1)) @ w2


def setup_inputs(seed: int = 0) -> dict:
    key = jax.random.key(seed)
    ks = iter(jax.random.split(key, 40))

    def nrm(shape, s):
        return jax.random.normal(next(ks), shape, F32) * s

    def gain(shape):
        return 1.0 + nrm(shape, 0.02)

    hq = N_HEADS * HEAD_DIM
    hkv = N_KV_HEADS * HEAD_DIM
    inp = {}
    inp['x'] = nrm((BATCH, SEQ, D_MODEL), 1.0)
    inp['c'] = nrm((BATCH, D_MODEL), 1.0)
    inp['ctx'] = nrm((BATCH, CTX_LEN, D_MODEL), 1.0)
    inp['c_ctx'] = nrm((D_MODEL,), 1.0)
    inp['mod_w'] = nrm((DEPTH, D_MODEL, 6 * D_MODEL), 0.3 * D_MODEL ** -0.5)
    inp['mod_b'] = nrm((DEPTH, 6 * D_MODEL), 0.01)
    inp['norm_mix_g'] = gain((DEPTH, D_MODEL))
    inp['norm_mlp_g'] = gain((DEPTH, D_MODEL))
    inp['mlp_w1'] = nrm((DEPTH, D_MODEL, D_FF), D_MODEL ** -0.5)
    inp['mlp_w2'] = nrm((DEPTH, D_FF, D_MODEL), D_FF ** -0.5)
    inp['attn_w_qkv'] = nrm((N_A, D_MODEL, hq + 2 * hkv), D_MODEL ** -0.5)
    inp['attn_q_gain'] = gain((N_A, HEAD_DIM))
    inp['attn_k_gain'] = gain((N_A, HEAD_DIM))
    inp['attn_w_o'] = nrm((N_A, hq, D_MODEL), hq ** -0.5)
    inp['conv_w_in'] = nrm((N_B, D_MODEL, 2 * D_MODEL), D_MODEL ** -0.5)
    inp['conv_b_in'] = nrm((N_B, 2 * D_MODEL), 0.02)
    inp['conv_w_dw'] = nrm((N_B, CONV_WIDTH, D_MODEL), CONV_WIDTH ** -0.5)
    inp['conv_b_dw'] = nrm((N_B, D_MODEL), 0.02)
    inp['conv_norm_g'] = gain((N_B, D_MODEL))
    inp['conv_norm_b'] = nrm((N_B, D_MODEL), 0.02)
    inp['conv_w_out'] = nrm((N_B, D_MODEL, D_MODEL), D_MODEL ** -0.5)
    inp['conv_b_out'] = nrm((N_B, D_MODEL), 0.02)
    inp['lru_w_in'] = nrm((N_C, D_MODEL, 2 * D_RNN), D_MODEL ** -0.5)
    inp['lru_conv_w'] = nrm((N_C, 2, LRU_CONV_WIDTH, D_RNN), LRU_CONV_WIDTH ** -0.5)
    inp['lru_conv_b'] = nrm((N_C, 2, D_RNN), 0.02)
    inp['lru_gate_w'] = nrm((N_C, 2, 2, N_LRU_BLOCKS, LRU_BLOCK, LRU_BLOCK), LRU_BLOCK ** -0.5)
    inp['lru_gate_b'] = nrm((N_C, 2, 2, D_RNN), 0.02)
    a8 = jax.random.uniform(next(ks), (N_C, 2, D_RNN), F32, 0.9, 0.999)
    a0 = a8 ** (1.0 / LRU_C)
    inp['lru_lambda'] = jnp.log(a0) - jnp.log1p(-a0)
    inp['lru_w_out'] = nrm((N_C, D_RNN, D_MODEL), D_RNN ** -0.5)
    return inp


def reference(x, c, ctx, c_ctx, mod_w, mod_b, norm_mix_g, norm_mlp_g, mlp_w1, mlp_w2,
              attn_w_qkv, attn_q_gain, attn_k_gain, attn_w_o,
              conv_w_in, conv_b_in, conv_w_dw, conv_b_dw, conv_norm_g, conv_norm_b, conv_w_out, conv_b_out,
              lru_w_in, lru_conv_w, lru_conv_b, lru_gate_w, lru_gate_b, lru_lambda, lru_w_out):
    x_lat, x_ctx = x, ctx
    cond_ctx = c_ctx[None, :]
    for i in range(DEPTH):
        kind = i % N_MIXERS
        j = i // N_MIXERS
        need_ctx = i < DEPTH - 1
        sh1, sc1, g1, sh2, sc2, g2 = ada_mod(c, mod_w[i], mod_b[i])
        csh1, csc1, cg1, csh2, csc2, cg2 = ada_mod(cond_ctx, mod_w[i], mod_b[i])
        h_l = modulate(rms_norm(x_lat, norm_mix_g[i]), sh1, sc1)
        h_c = modulate(rms_norm(x_ctx, norm_mix_g[i]), csh1, csc1)
        if kind == 0:
            out_l, out_c = attention_mixer(h_l, h_c, attn_w_qkv[j], attn_q_gain[j], attn_k_gain[j],
                                           attn_w_o[j], need_ctx)
        elif kind == 1:
            out_l, out_c = conformer_mixer(h_l, h_c, conv_w_in[j], conv_b_in[j], conv_w_dw[j], conv_b_dw[j],
                                           conv_norm_g[j], conv_norm_b[j], conv_w_out[j], conv_b_out[j], need_ctx)
        else:
            out_l, out_c = recurrent_mixer(h_l, h_c, lru_w_in[j], lru_conv_w[j], lru_conv_b[j], lru_gate_w[j],
                                           lru_gate_b[j], lru_lambda[j], lru_w_out[j], need_ctx)
        x_lat = x_lat + g1 * out_l
        x_lat = x_lat + g2 * sq_relu_mlp(modulate(rms_norm(x_lat, norm_mlp_g[i]), sh2, sc2), mlp_w1[i], mlp_w2[i])
        if need_ctx:
            x_ctx = x_ctx + cg1 * out_c
            x_ctx = x_ctx + cg2 * sq_relu_mlp(modulate(rms_norm(x_ctx, norm_mlp_g[i]), csh2, csc2),
                                              mlp_w1[i], mlp_w2[i])
    return x_lat
```

```python
import functools

import jax
import jax.numpy as jnp
from jax import lax
from jax.experimental import pallas as pl
from jax.experimental.pallas import tpu as pltpu

D_MODEL = 1024
BATCH = 8
SEQ = 2048
DEPTH = 4
GRID_W = 64
CTX_LEN = 256
N_MIXERS = 3
N_HEADS = 16
N_KV_HEADS = 4
HEAD_DIM = D_MODEL // N_HEADS
Q_GROUP = N_HEADS // N_KV_HEADS
ROPE_FREQS = HEAD_DIM // 4
ROPE_THETA = 10000.0
CONV_WIDTH = 31
D_RNN = D_MODEL
N_LRU_BLOCKS = 4
LRU_BLOCK = D_RNN // N_LRU_BLOCKS
LRU_CONV_WIDTH = 4
LRU_C = 8.0
D_FF = 4 * D_MODEL
EPS = 1e-6

ROWS = CTX_LEN + SEQ
HQ = N_HEADS * HEAD_DIM
HKV = N_KV_HEADS * HEAD_DIM
COND_ROWS = 16
CTX_COND = BATCH
LANES = 128
MXU_DIM = 256

TM = 768
TILES_PER_BATCH = ROWS // TM
TQ = 256
Q_TILES = ROWS // TQ
TF = 1024
T_CHUNK = 256
N_CHUNKS = ROWS // T_CHUNK
HALO = 16
LRU_HALO = 8
VMEM_LIMIT = 56 * 1024 * 1024

F32 = jnp.float32
BF16 = jnp.bfloat16


def _dot(a, b):
    return jnp.dot(a, b, preferred_element_type=F32)


def _params(*sem):
    return pltpu.CompilerParams(dimension_semantics=sem, vmem_limit_bytes=VMEM_LIMIT)


def _ada_kernel(cond_ref, w_ref, b_ref, out_ref):
    c = cond_ref[...]
    s = (c * jax.nn.sigmoid(c)).astype(BF16)
    out_ref[...] = _dot(s, w_ref[...].astype(BF16)) + b_ref[...]


def _ada_mod(cond, mod_w, mod_b):
    tn = 1536
    out = pl.pallas_call(
        _ada_kernel,
        grid=(DEPTH, 6 * D_MODEL // tn),
        in_specs=[
            pl.BlockSpec((COND_ROWS, D_MODEL), lambda l, n: (0, 0)),
            pl.BlockSpec((None, D_MODEL, tn), lambda l, n: (l, 0, n)),
            pl.BlockSpec((None, 1, tn), lambda l, n: (l, 0, n)),
        ],
        out_specs=pl.BlockSpec((None, COND_ROWS, tn), lambda l, n: (l, 0, n)),
        out_shape=jax.ShapeDtypeStruct((DEPTH, COND_ROWS, 6 * D_MODEL), F32),
        compiler_params=_params("arbitrary", "arbitrary"),
        name="ada_mod",
    )(cond, mod_w, mod_b.reshape(DEPTH, 1, 6 * D_MODEL))
    return out.reshape(DEPTH, COND_ROWS, 6, 1, D_MODEL)


def _mod_specs(layer, which):
    blk = (None, None, None, 1, D_MODEL)
    return [
        pl.BlockSpec(blk, lambda b, j: (layer, b, which, 0, 0)),
        pl.BlockSpec(blk, lambda b, j: (layer, CTX_COND, which, 0, 0)),
    ]


def _ctx_rows(tm, j):
    rows = lax.broadcasted_iota(jnp.int32, (tm, 1), 0)
    return jnp.logical_and(rows < CTX_LEN, j == 0)


def _norm_mod(x, g, is_ctx, sh, csh, sc, csc):
    ms = jnp.mean(x * x, axis=-1, keepdims=True)
    h = x * lax.rsqrt(ms + EPS) * g
    scale = jnp.where(is_ctx, csc, sc)
    shift = jnp.where(is_ctx, csh, sh)
    return h * (1.0 + scale) + shift


def _head_norm(z, gain, ones_blk):
    ms = _dot((z * z).astype(BF16), ones_blk)
    return z * lax.rsqrt(ms + EPS) * gain


def _rope(z, c, s_up, s_dn):
    up = pltpu.roll(z, LANES - ROPE_FREQS, axis=1)
    dn = pltpu.roll(z, ROPE_FREQS, axis=1)
    return z * c + up * s_up + dn * s_dn


def _qkv_kernel(x_ref, g_ref, sh_ref, csh_ref, sc_ref, csc_ref, w_ref, qg_ref, kg_ref, ones_ref,
                c_ref, sup_ref, sdn_ref, q_ref, k_ref, v_ref):
    j = pl.program_id(1)
    is_ctx = _ctx_rows(TM, j)
    h = _norm_mod(x_ref[...], g_ref[...], is_ctx, sh_ref[...], csh_ref[...], sc_ref[...], csc_ref[...])
    acc = _dot(h.astype(BF16), w_ref[...])
    ones_blk = ones_ref[...]
    c, s_up, s_dn = c_ref[...], sup_ref[...], sdn_ref[...]
    for blk in range((HQ + HKV) // MXU_DIM):
        is_q = blk < HQ // MXU_DIM
        z = acc[:, blk * MXU_DIM:(blk + 1) * MXU_DIM]
        zn = _head_norm(z, qg_ref[...] if is_q else kg_ref[...], ones_blk)
        for half in range(MXU_DIM // LANES):
            r = _rope(zn[:, half * LANES:(half + 1) * LANES], c, s_up, s_dn)
            if is_q:
                col = blk * MXU_DIM + half * LANES
                q_ref[:, col:col + LANES] = (r * (HEAD_DIM ** -0.5)).astype(BF16)
            else:
                col = half * LANES
                k_ref[:, col:col + LANES] = r.astype(BF16)
    v_ref[...] = acc[:, HQ + HKV:].astype(BF16)


def _rope_tables():
    rows = SEQ // GRID_W
    row = jnp.repeat(jnp.arange(rows, dtype=jnp.int32), GRID_W)
    col = jnp.tile(jnp.arange(GRID_W, dtype=jnp.int32), rows)
    pos = jnp.stack([row, col], axis=-1).astype(F32)
    inv = ROPE_THETA ** (-jnp.arange(ROPE_FREQS, dtype=F32) / ROPE_FREQS)
    ang = pos[:, :, None] * inv
    cos, sin = jnp.cos(ang), jnp.sin(ang)
    zero = jnp.zeros_like(sin)
    c = jnp.concatenate([cos, cos], axis=-1).reshape(SEQ, HEAD_DIM)
    s_up = jnp.concatenate([-sin, zero], axis=-1).reshape(SEQ, HEAD_DIM)
    s_dn = jnp.concatenate([zero, sin], axis=-1).reshape(SEQ, HEAD_DIM)

    def full(t, ctx_val):
        t = jnp.tile(t, (1, LANES // HEAD_DIM))
        return jnp.concatenate([jnp.full((CTX_LEN, LANES), ctx_val, F32), t], axis=0)

    return full(c, 1.0), full(s_up, 0.0), full(s_dn, 0.0)


def _qkv_proj(x, mods, layer, norm_g, w_qkv, q_gain, k_gain, tables):
    n = BATCH * ROWS
    ones_blk = (jnp.kron(jnp.eye(MXU_DIM // HEAD_DIM, dtype=F32), jnp.ones((HEAD_DIM, HEAD_DIM), F32))
                / HEAD_DIM).astype(BF16)
    reps = MXU_DIM // HEAD_DIM
    row = lambda b, j: (b * TILES_PER_BATCH + j, 0)
    const = lambda b, j: (0, 0)
    tab = pl.BlockSpec((TM, LANES), lambda b, j: (j, 0))
    return pl.pallas_call(
        _qkv_kernel,
        grid=(BATCH, TILES_PER_BATCH),
        in_specs=[
            pl.BlockSpec((TM, D_MODEL), row),
            pl.BlockSpec((1, D_MODEL), const),
            *_mod_specs(layer, 0), *_mod_specs(layer, 1),
            pl.BlockSpec((D_MODEL, HQ + 2 * HKV), const),
            pl.BlockSpec((1, MXU_DIM), const),
            pl.BlockSpec((1, MXU_DIM), const),
            pl.BlockSpec((MXU_DIM, MXU_DIM), const),
            tab, tab, tab,
        ],
        out_specs=[
            pl.BlockSpec((TM, HQ), row),
            pl.BlockSpec((TM, HKV), row),
            pl.BlockSpec((TM, HKV), row),
        ],
        out_shape=[
            jax.ShapeDtypeStruct((n, HQ), BF16),
            jax.ShapeDtypeStruct((n, HKV), BF16),
            jax.ShapeDtypeStruct((n, HKV), BF16),
        ],
        compiler_params=_params("arbitrary", "arbitrary"),
        name="qkv_proj",
    )(x, norm_g.reshape(1, D_MODEL), mods, mods, mods, mods, w_qkv.astype(BF16),
      jnp.tile(q_gain, reps).reshape(1, MXU_DIM), jnp.tile(k_gain, reps).reshape(1, MXU_DIM),
      ones_blk, *tables)


def _attend(q_ref, k_ref, v_ref, o_ref, n_keys):
    for h in range(N_HEADS):
        kv = h // Q_GROUP
        qh = q_ref[:, h * HEAD_DIM:(h + 1) * HEAD_DIM]
        kh = k_ref[0:n_keys, kv * HEAD_DIM:(kv + 1) * HEAD_DIM]
        vh = v_ref[0:n_keys, kv * HEAD_DIM:(kv + 1) * HEAD_DIM]
        s = lax.dot_general(qh, kh, (((1,), (1,)), ((), ())), preferred_element_type=F32)
        m = jnp.max(s, axis=-1, keepdims=True)
        p = jnp.exp(s - m)
        l = jnp.sum(p, axis=-1, keepdims=True)
        o = _dot(p.astype(BF16), vh) / l
        o_ref[:, h * HEAD_DIM:(h + 1) * HEAD_DIM] = o.astype(o_ref.dtype)


def _attn_kernel(q_ref, k_ref, v_ref, o_ref):
    r = pl.program_id(1)

    @pl.when(r == 0)
    def _():
        _attend(q_ref, k_ref, v_ref, o_ref, CTX_LEN)

    @pl.when(r != 0)
    def _():
        _attend(q_ref, k_ref, v_ref, o_ref, ROWS)


def _attention(q, k, v):
    kv_spec = pl.BlockSpec((ROWS, HKV), lambda b, r: (b, 0))
    q_spec = pl.BlockSpec((TQ, HQ), lambda b, r: (b * Q_TILES + r, 0))
    return pl.pallas_call(
        _attn_kernel,
        grid=(BATCH, Q_TILES),
        in_specs=[q_spec, kv_spec, kv_spec],
        out_specs=q_spec,
        out_shape=jax.ShapeDtypeStruct((BATCH * ROWS, HQ), BF16),
        compiler_params=_params("arbitrary", "arbitrary"),
        name="attention",
    )(q, k, v)


def _tail_kernel(a_ref, x_ref, wm_ref, bm_ref, g1_ref, cg1_ref, ng_ref, sh_ref, csh_ref, sc_ref, csc_ref,
                 g2_ref, cg2_ref, w1_ref, w2_ref, o_ref, xn_s, h_s, acc_s):
    j = pl.program_id(1)
    k = pl.program_id(2)
    is_ctx = _ctx_rows(TM, j)

    @pl.when(k == 0)
    def _():
        mix = _dot(a_ref[...], wm_ref[...]) + bm_ref[...]
        xn = x_ref[...] + jnp.where(is_ctx, cg1_ref[...], g1_ref[...]) * mix
        xn_s[...] = xn
        h = _norm_mod(xn, ng_ref[...], is_ctx, sh_ref[...], csh_ref[...], sc_ref[...], csc_ref[...])
        h_s[...] = h.astype(BF16)

    u = jnp.maximum(_dot(h_s[...], w1_ref[...]), 0.0)
    part = _dot((u * u).astype(BF16), w2_ref[...])

    @pl.when(k == 0)
    def _():
        acc_s[...] = part

    @pl.when(k != 0)
    def _():
        acc_s[...] += part

    @pl.when(k == D_FF // TF - 1)
    def _():
        o_ref[...] = xn_s[...] + jnp.where(is_ctx, cg2_ref[...], g2_ref[...]) * acc_s[...]


def _tail(a, x, mods, layer, w_mix, b_mix, norm_g, w1, w2):
    row = lambda b, j, k: (b * TILES_PER_BATCH + j, 0)
    const = lambda b, j, k: (0, 0)

    def mod(which):
        blk = (None, None, None, 1, D_MODEL)
        return [pl.BlockSpec(blk, lambda b, j, k: (layer, b, which, 0, 0)),
                pl.BlockSpec(blk, lambda b, j, k: (layer, CTX_COND, which, 0, 0))]

    return pl.pallas_call(
        _tail_kernel,
        grid=(BATCH, TILES_PER_BATCH, D_FF // TF),
        in_specs=[
            pl.BlockSpec((TM, D_MODEL), row),
            pl.BlockSpec((TM, D_MODEL), row),
            pl.BlockSpec((D_MODEL, D_MODEL), const),
            pl.BlockSpec((1, D_MODEL), const),
            *mod(2),
            pl.BlockSpec((1, D_MODEL), const),
            *mod(3), *mod(4), *mod(5),
            pl.BlockSpec((D_MODEL, TF), lambda b, j, k: (0, k)),
            pl.BlockSpec((TF, D_MODEL), lambda b, j, k: (k, 0)),
        ],
        out_specs=pl.BlockSpec((TM, D_MODEL), row),
        out_shape=jax.ShapeDtypeStruct((BATCH * ROWS, D_MODEL), F32),
        scratch_shapes=[
            pltpu.VMEM((TM, D_MODEL), F32),
            pltpu.VMEM((TM, D_MODEL), BF16),
            pltpu.VMEM((TM, D_MODEL), F32),
        ],
        compiler_params=_params("arbitrary", "arbitrary", "arbitrary"),
        name="tail",
    )(a, x, w_mix.astype(BF16), b_mix.reshape(1, D_MODEL), mods, mods, norm_g.reshape(1, D_MODEL),
      mods, mods, mods, mods, mods, mods, w1.astype(BF16), w2.astype(BF16))


def _glu_kernel(x_ref, g_ref, sh_ref, csh_ref, sc_ref, csc_ref, w_ref, b_ref, u_ref):
    j = pl.program_id(1)
    is_ctx = _ctx_rows(TM, j)
    h = _norm_mod(x_ref[...], g_ref[...], is_ctx, sh_ref[...], csh_ref[...], sc_ref[...], csc_ref[...])
    acc = _dot(h.astype(BF16), w_ref[...]) + b_ref[...]
    u_ref[...] = acc[:, :D_MODEL] * jax.nn.sigmoid(acc[:, D_MODEL:])


def _gelu_kernel(x_ref, g_ref, sh_ref, csh_ref, sc_ref, csc_ref, w_ref, gate_ref, xr_ref):
    j = pl.program_id(1)
    is_ctx = _ctx_rows(TM, j)
    h = _norm_mod(x_ref[...], g_ref[...], is_ctx, sh_ref[...], csh_ref[...], sc_ref[...], csc_ref[...])
    acc = _dot(h.astype(BF16), w_ref[...])
    gate_ref[...] = jax.nn.gelu(acc[:, :D_RNN])
    xr_ref[...] = acc[:, D_RNN:]


def _in_proj(kernel, name, x, mods, layer, norm_g, w, bias, n_out):
    row = lambda b, j: (b * TILES_PER_BATCH + j, 0)
    const = lambda b, j: (0, 0)
    extra_specs = [] if bias is None else [pl.BlockSpec((1, 2 * D_MODEL), const)]
    extra_args = [] if bias is None else [bias.reshape(1, 2 * D_MODEL)]
    out_spec = pl.BlockSpec((TM, D_MODEL), row)
    out_shape = jax.ShapeDtypeStruct((BATCH * ROWS, D_MODEL), F32)
    return pl.pallas_call(
        kernel,
        grid=(BATCH, TILES_PER_BATCH),
        in_specs=[
            pl.BlockSpec((TM, D_MODEL), row),
            pl.BlockSpec((1, D_MODEL), const),
            *_mod_specs(layer, 0), *_mod_specs(layer, 1),
            pl.BlockSpec((D_MODEL, 2 * D_MODEL), const),
            *extra_specs,
        ],
        out_specs=out_spec if n_out == 1 else [out_spec] * n_out,
        out_shape=out_shape if n_out == 1 else [out_shape] * n_out,
        compiler_params=_params("arbitrary", "arbitrary"),
        name=name,
    )(x, norm_g.reshape(1, D_MODEL), mods, mods, mods, mods, w.astype(BF16), *extra_args)


CONV_ROWS = 64


def _conv_kernel(prev_ref, cur_ref, next_ref, w_ref, b_ref, ng_ref, nb_ref, a_ref, buf_s, y_s):
    r = pl.program_id(1)
    half = CONV_WIDTH // 2
    prev_ok = (r >= 2).astype(F32)
    next_ok = jnp.logical_and(r >= 1, r <= Q_TILES - 2).astype(F32)
    buf_s[0:HALO, :] = prev_ref[...] * prev_ok
    buf_s[HALO:HALO + TQ, :] = cur_ref[...]
    buf_s[HALO + TQ:HALO + TQ + HALO, :] = next_ref[...] * next_ok
    for c in range(D_MODEL // LANES):
        cols = slice(c * LANES, (c + 1) * LANES)
        w = w_ref[:, cols]
        for rb in range(TQ // CONV_ROWS):
            base = HALO - half + rb * CONV_ROWS
            acc = jnp.zeros((CONV_ROWS, LANES), F32)
            for tap in range(CONV_WIDTH):
                acc = acc + w[tap:tap + 1, :] * buf_s[base + tap:base + tap + CONV_ROWS, cols]
            y_s[rb * CONV_ROWS:(rb + 1) * CONV_ROWS, cols] = acc
    y = y_s[...] + b_ref[...]
    yc = y - jnp.mean(y, axis=-1, keepdims=True)
    var = jnp.mean(yc * yc, axis=-1, keepdims=True)
    z = yc * lax.rsqrt(var + EPS) * ng_ref[...] + nb_ref[...]
    a_ref[...] = (z * jax.nn.sigmoid(z)).astype(a_ref.dtype)


def _conv_module(u, w_dw, b_dw, norm_g, norm_b):
    per = TQ // HALO
    last = BATCH * ROWS // HALO - 1
    tile = lambda b, r: b * Q_TILES + r
    const = lambda b, r: (0, 0)
    return pl.pallas_call(
        _conv_kernel,
        grid=(BATCH, Q_TILES),
        in_specs=[
            pl.BlockSpec((HALO, D_MODEL), lambda b, r: (jnp.maximum(tile(b, r) * per - 1, 0), 0)),
            pl.BlockSpec((TQ, D_MODEL), lambda b, r: (tile(b, r), 0)),
            pl.BlockSpec((HALO, D_MODEL), lambda b, r: (jnp.minimum((tile(b, r) + 1) * per, last), 0)),
            pl.BlockSpec((CONV_WIDTH, D_MODEL), const),
            pl.BlockSpec((1, D_MODEL), const),
            pl.BlockSpec((1, D_MODEL), const),
            pl.BlockSpec((1, D_MODEL), const),
        ],
        out_specs=pl.BlockSpec((TQ, D_MODEL), lambda b, r: (tile(b, r), 0)),
        out_shape=jax.ShapeDtypeStruct((BATCH * ROWS, D_MODEL), BF16),
        scratch_shapes=[
            pltpu.VMEM((TQ + 2 * HALO, D_MODEL), F32),
            pltpu.VMEM((TQ, D_MODEL), F32),
        ],
        compiler_params=_params("arbitrary", "arbitrary"),
        name="conv_module",
    )(u, u, u, w_dw, b_dw.reshape(1, D_MODEL), norm_g.reshape(1, D_MODEL), norm_b.reshape(1, D_MODEL))


def _lru_kernel(reverse, x_ref, halo_ref, cw_ref, cb_ref, wr_ref, wi_ref, br_ref, bi_ref, lam_ref, *rest):
    if reverse:
        hf_ref, gate_ref, out_ref, xp_s, a_s, b_s, h_s = rest
    else:
        out_ref, xp_s, a_s, b_s, h_s = rest
    i = pl.program_id(1)
    chunk = jnp.where(i == 0, 0, N_CHUNKS - i) if reverse else i
    n = BATCH * T_CHUNK
    w = LRU_CONV_WIDTH

    if reverse:
        halo_ok = jnp.logical_and(chunk >= 1, chunk <= N_CHUNKS - 2).astype(F32)
        xp_s[:, 0:T_CHUNK, :] = x_ref[...]
        xp_s[:, T_CHUNK:T_CHUNK + LRU_HALO, :] = halo_ref[...] * halo_ok
        shifted = lambda s: xp_s[:, s:s + T_CHUNK, :]
    else:
        halo_ok = (chunk >= 2).astype(F32)
        xp_s[:, 0:LRU_HALO, :] = halo_ref[...] * halo_ok
        xp_s[:, LRU_HALO:LRU_HALO + T_CHUNK, :] = x_ref[...]
        shifted = lambda s: xp_s[:, LRU_HALO - s:LRU_HALO - s + T_CHUNK, :]
    cw = cw_ref[...]
    u = cb_ref[...] + cw[w - 1:w, :] * shifted(0)
    for s in range(1, w):
        u = u + cw[w - 1 - s:w - s, :] * shifted(s)
    u = u.reshape(n, LRU_BLOCK)

    ub = u.astype(BF16)
    r_gate = jax.nn.sigmoid(_dot(ub, wr_ref[...]) + br_ref[...])
    i_gate = jax.nn.sigmoid(_dot(ub, wi_ref[...]) + bi_ref[...])
    log_a = -LRU_C * r_gate * jax.nn.softplus(-lam_ref[...])
    a = jnp.exp(log_a)
    mult = jnp.sqrt(-jnp.tanh(log_a) * (a * a + 1.0))
    t_idx = lax.broadcasted_iota(jnp.int32, (n, 1), 0) % T_CHUNK
    start = jnp.logical_and(t_idx == (T_CHUNK - 1 if reverse else 0), i == 0)
    mult = jnp.where(start, 1.0, mult)
    b = mult * i_gate * u
    for c in range(LRU_BLOCK // LANES):
        a_s[c] = a[:, c * LANES:(c + 1) * LANES]
        b_s[c] = b[:, c * LANES:(c + 1) * LANES]

    @pl.when(i == 0)
    def _():
        h_s[...] = jnp.zeros_like(h_s)

    def step(k, hs):
        t = T_CHUNK - 1 - k if reverse else k
        rows = pl.ds(t, BATCH, stride=T_CHUNK)
        out = []
        for c, h in enumerate(hs):
            h = a_s[c, rows, :] * h + b_s[c, rows, :]
            b_s[c, rows, :] = h
            out.append(h)
        return tuple(out)

    h0 = tuple(h_s[c] for c in range(LRU_BLOCK // LANES))
    h1 = lax.fori_loop(0, T_CHUNK, step, h0, unroll=8)
    for c, h in enumerate(h1):
        h_s[c] = h

    hs = jnp.concatenate([b_s[c] for c in range(LRU_BLOCK // LANES)], axis=-1)
    hs = hs.reshape(BATCH, T_CHUNK, LRU_BLOCK)
    if reverse:
        out_ref[...] = ((hf_ref[...] + hs) * gate_ref[...]).astype(out_ref.dtype)
    else:
        out_ref[...] = hs


def _lru_scan(reverse, xr, conv_w, conv_b, gate_w, gate_b, lam, hf=None, gate=None):
    per = T_CHUNK // LRU_HALO
    last = ROWS // LRU_HALO - 1
    if reverse:
        chunk = lambda i: jnp.where(i == 0, 0, N_CHUNKS - i)
        halo = lambda n, i: (0, jnp.minimum((chunk(i) + 1) * per, last), n)
    else:
        chunk = lambda i: i
        halo = lambda n, i: (0, jnp.maximum(chunk(i) * per - 1, 0), n)
    blk = pl.BlockSpec((BATCH, T_CHUNK, LRU_BLOCK), lambda n, i: (0, chunk(i), n))
    vec = pl.BlockSpec((1, LRU_BLOCK), lambda n, i: (0, n))
    mat = pl.BlockSpec((None, LRU_BLOCK, LRU_BLOCK), lambda n, i: (n, 0, 0))
    extra_specs = [blk, blk] if reverse else []
    extra_args = [hf, gate] if reverse else []
    return pl.pallas_call(
        functools.partial(_lru_kernel, reverse),
        grid=(N_LRU_BLOCKS, N_CHUNKS),
        in_specs=[
            blk,
            pl.BlockSpec((BATCH, LRU_HALO, LRU_BLOCK), halo),
            pl.BlockSpec((LRU_CONV_WIDTH, LRU_BLOCK), lambda n, i: (0, n)),
            vec, mat, mat, vec, vec, vec,
            *extra_specs,
        ],
        out_specs=blk,
        out_shape=jax.ShapeDtypeStruct((BATCH, ROWS, D_RNN), BF16 if reverse else F32),
        scratch_shapes=[
            pltpu.VMEM((BATCH, T_CHUNK + LRU_HALO, LRU_BLOCK), F32),
            pltpu.VMEM((LRU_BLOCK // LANES, BATCH * T_CHUNK, LANES), F32),
            pltpu.VMEM((LRU_BLOCK // LANES, BATCH * T_CHUNK, LANES), F32),
            pltpu.VMEM((LRU_BLOCK // LANES, BATCH, LANES), F32),
        ],
        compiler_params=_params("arbitrary", "arbitrary"),
        name="lru_bwd" if reverse else "lru_fwd",
    )(xr, xr, conv_w, conv_b.reshape(1, D_RNN), gate_w[0].astype(BF16), gate_w[1].astype(BF16),
      gate_b[0].reshape(1, D_RNN), gate_b[1].reshape(1, D_RNN), lam.reshape(1, D_RNN), *extra_args)


def kernel(x, c, ctx, c_ctx, mod_w, mod_b, norm_mix_g, norm_mlp_g, mlp_w1, mlp_w2, attn_w_qkv, attn_q_gain,
           attn_k_gain, attn_w_o, conv_w_in, conv_b_in, conv_w_dw, conv_b_dw, conv_norm_g, conv_norm_b,
           conv_w_out, conv_b_out, lru_w_in, lru_conv_w, lru_conv_b, lru_gate_w, lru_gate_b, lru_lambda,
           lru_w_out):
    cond = jnp.concatenate([c, c_ctx[None, :], jnp.zeros((COND_ROWS - BATCH - 1, D_MODEL), F32)], axis=0)
    mods = _ada_mod(cond, mod_w, mod_b)
    xs = jnp.concatenate([ctx, x], axis=1).reshape(BATCH * ROWS, D_MODEL)
    tables = _rope_tables()
    no_bias = jnp.zeros((D_MODEL,), F32)
    for i in range(DEPTH):
        kind, j = i % N_MIXERS, i // N_MIXERS
        if kind == 0:
            q, k, v = _qkv_proj(xs, mods, i, norm_mix_g[i], attn_w_qkv[j], attn_q_gain[j], attn_k_gain[j], tables)
            a = _attention(q, k, v)
            w_mix, b_mix = attn_w_o[j], no_bias
        elif kind == 1:
            u = _in_proj(_glu_kernel, "conv_in", xs, mods, i, norm_mix_g[i], conv_w_in[j], conv_b_in[j], 1)
            a = _conv_module(u, conv_w_dw[j], conv_b_dw[j], conv_norm_g[j], conv_norm_b[j])
            w_mix, b_mix = conv_w_out[j], conv_b_out[j]
        else:
            gate, xr = _in_proj(_gelu_kernel, "lru_in", xs, mods, i, norm_mix_g[i], lru_w_in[j], None, 2)
            gate = gate.reshape(BATCH, ROWS, D_RNN)
            xr = xr.reshape(BATCH, ROWS, D_RNN)
            hf = _lru_scan(False, xr, lru_conv_w[j, 0], lru_conv_b[j, 0], lru_gate_w[j, 0], lru_gate_b[j, 0],
                           lru_lambda[j, 0])
            a = _lru_scan(True, xr, lru_conv_w[j, 1], lru_conv_b[j, 1], lru_gate_w[j, 1], lru_gate_b[j, 1],
                          lru_lambda[j, 1], hf, gate)
            a = a.reshape(BATCH * ROWS, D_RNN)
            w_mix, b_mix = lru_w_out[j], no_bias
        xs = _tail(a, xs, mods, i, w_mix, b_mix, norm_mlp_g[i], mlp_w1[i], mlp_w2[i])
    return xs.reshape(BATCH, ROWS, D_MODEL)[:, CTX_LEN:]
```

```python
import functools

import jax
import jax.numpy as jnp
from jax import lax
from jax.experimental import pallas as pl
from jax.experimental.pallas import tpu as pltpu

D_MODEL = 1024
BATCH = 8
SEQ = 2048
DEPTH = 4
GRID_W = 64
CTX_LEN = 256
N_MIXERS = 3
N_HEADS = 16
N_KV_HEADS = 4
HEAD_DIM = D_MODEL // N_HEADS
Q_GROUP = N_HEADS // N_KV_HEADS
ROPE_FREQS = HEAD_DIM // 4
ROPE_THETA = 10000.0
CONV_WIDTH = 31
D_RNN = D_MODEL
N_LRU_BLOCKS = 4
LRU_BLOCK = D_RNN // N_LRU_BLOCKS
LRU_CONV_WIDTH = 4
LRU_C = 8.0
D_FF = 4 * D_MODEL
EPS = 1e-6
LOG2E = 1.4426950408889634
Q_SCALE = HEAD_DIM ** -0.5 * LOG2E

ROWS = CTX_LEN + SEQ
HQ = N_HEADS * HEAD_DIM
HKV = N_KV_HEADS * HEAD_DIM
COND_ROWS = 16
CTX_COND = BATCH
LANES = 128
MXU_DIM = 256

TM = 768
TILES_PER_BATCH = ROWS // TM
TM_LAT = 512
TQ = 256
Q_TILES = ROWS // TQ
TF = 1024
T_CHUNK = 256
N_CHUNKS = ROWS // T_CHUNK
HALO = 16
LRU_HALO = 8
LRU_PITCH = T_CHUNK + LRU_HALO
VMEM_LIMIT = 56 * 1024 * 1024

F32 = jnp.float32
BF16 = jnp.bfloat16


def _dot(a, b):
    return jnp.dot(a, b, preferred_element_type=F32)


def _sigmoid(x):
    return 0.5 * jnp.tanh(0.5 * x) + 0.5


def _params(*sem):
    return pltpu.CompilerParams(dimension_semantics=sem, vmem_limit_bytes=VMEM_LIMIT)


def _ada_kernel(cond_ref, w_ref, b_ref, out_ref):
    c = cond_ref[...]
    s = (c * _sigmoid(c)).astype(BF16)
    out_ref[...] = _dot(s, w_ref[...].astype(BF16)) + b_ref[...]


def _ada_mod(cond, mod_w, mod_b):
    tn = 1536
    out = pl.pallas_call(
        _ada_kernel,
        grid=(DEPTH, 6 * D_MODEL // tn),
        in_specs=[
            pl.BlockSpec((COND_ROWS, D_MODEL), lambda l, n: (0, 0)),
            pl.BlockSpec((None, D_MODEL, tn), lambda l, n: (l, 0, n)),
            pl.BlockSpec((None, 1, tn), lambda l, n: (l, 0, n)),
        ],
        out_specs=pl.BlockSpec((None, COND_ROWS, tn), lambda l, n: (l, 0, n)),
        out_shape=jax.ShapeDtypeStruct((DEPTH, COND_ROWS, 6 * D_MODEL), F32),
        compiler_params=_params("arbitrary", "arbitrary"),
        name="ada_mod",
    )(cond, mod_w, mod_b.reshape(DEPTH, 1, 6 * D_MODEL))
    return out.reshape(DEPTH, COND_ROWS, 6, 1, D_MODEL)


def _mod_specs(layer, which):
    blk = (None, None, None, 1, D_MODEL)
    return [
        pl.BlockSpec(blk, lambda b, j: (layer, b, which, 0, 0)),
        pl.BlockSpec(blk, lambda b, j: (layer, CTX_COND, which, 0, 0)),
    ]


def _ctx_rows(tm, j):
    rows = lax.broadcasted_iota(jnp.int32, (tm, 1), 0)
    return jnp.logical_and(rows < CTX_LEN, j == 0)


def _norm_mod(x, g, is_ctx, sh, csh, sc, csc):
    ms = jnp.mean(x * x, axis=-1, keepdims=True)
    h = x * lax.rsqrt(ms + EPS) * g
    scale = jnp.where(is_ctx, csc, sc)
    shift = jnp.where(is_ctx, csh, sh)
    return h * (1.0 + scale) + shift


def _head_norm(z, gain, ones_blk):
    ms = _dot((z * z).astype(BF16), ones_blk)
    return z * lax.rsqrt(ms + EPS) * gain


def _rope(z, c, s_up, s_dn):
    up = pltpu.roll(z, LANES - ROPE_FREQS, axis=1)
    dn = pltpu.roll(z, ROPE_FREQS, axis=1)
    return z * c + up * s_up + dn * s_dn


def _qkv_kernel(x_ref, g_ref, sh_ref, csh_ref, sc_ref, csc_ref, w_ref, qg_ref, kg_ref, ones_ref,
                c_ref, sup_ref, sdn_ref, qt_ref, k_ref, vt_ref):
    j = pl.program_id(1)
    is_ctx = _ctx_rows(TM, j)
    h = _norm_mod(x_ref[...], g_ref[...], is_ctx, sh_ref[...], csh_ref[...], sc_ref[...], csc_ref[...])
    h = h.astype(BF16)
    ones_blk = ones_ref[...]
    c, s_up, s_dn = c_ref[...], sup_ref[...], sdn_ref[...]
    n_blk = (HQ + 2 * HKV) // MXU_DIM
    proj = lambda blk: _dot(h, w_ref[:, blk * MXU_DIM:(blk + 1) * MXU_DIM])
    z_next = proj(0)
    for blk in range(n_blk):
        z = z_next
        if blk + 1 < n_blk:
            z_next = proj(blk + 1)
        if blk >= (HQ + HKV) // MXU_DIM:
            vt_ref[...] = z.astype(BF16).T
            continue
        is_q = blk < HQ // MXU_DIM
        zn = _head_norm(z, qg_ref[...] if is_q else kg_ref[...], ones_blk)
        for half in range(MXU_DIM // LANES):
            r = _rope(zn[:, half * LANES:(half + 1) * LANES], c, s_up, s_dn)
            if is_q:
                col = blk * MXU_DIM + half * LANES
                qt_ref[col:col + LANES, :] = (r * Q_SCALE).T.astype(BF16)
            else:
                col = half * LANES
                k_ref[:, col:col + LANES] = r.astype(BF16)


def _rope_tables():
    rows = SEQ // GRID_W
    row = jnp.repeat(jnp.arange(rows, dtype=jnp.int32), GRID_W)
    col = jnp.tile(jnp.arange(GRID_W, dtype=jnp.int32), rows)
    pos = jnp.stack([row, col], axis=-1).astype(F32)
    inv = ROPE_THETA ** (-jnp.arange(ROPE_FREQS, dtype=F32) / ROPE_FREQS)
    ang = pos[:, :, None] * inv
    cos, sin = jnp.cos(ang), jnp.sin(ang)
    zero = jnp.zeros_like(sin)
    c = jnp.concatenate([cos, cos], axis=-1).reshape(SEQ, HEAD_DIM)
    s_up = jnp.concatenate([-sin, zero], axis=-1).reshape(SEQ, HEAD_DIM)
    s_dn = jnp.concatenate([zero, sin], axis=-1).reshape(SEQ, HEAD_DIM)

    def full(t, ctx_val):
        t = jnp.tile(t, (1, LANES // HEAD_DIM))
        return jnp.concatenate([jnp.full((CTX_LEN, LANES), ctx_val, F32), t], axis=0)

    return full(c, 1.0), full(s_up, 0.0), full(s_dn, 0.0)


def _qkv_proj(x, mods, layer, norm_g, w_qkv, q_gain, k_gain, tables):
    n = BATCH * ROWS
    ones_blk = (jnp.kron(jnp.eye(MXU_DIM // HEAD_DIM, dtype=F32), jnp.ones((HEAD_DIM, HEAD_DIM), F32))
                / HEAD_DIM).astype(BF16)
    reps = MXU_DIM // HEAD_DIM
    row = lambda b, j: (b * TILES_PER_BATCH + j, 0)
    const = lambda b, j: (0, 0)
    tab = pl.BlockSpec((TM, LANES), lambda b, j: (j, 0))
    return pl.pallas_call(
        _qkv_kernel,
        grid=(BATCH, TILES_PER_BATCH),
        in_specs=[
            pl.BlockSpec((TM, D_MODEL), row),
            pl.BlockSpec((1, D_MODEL), const),
            *_mod_specs(layer, 0), *_mod_specs(layer, 1),
            pl.BlockSpec((D_MODEL, HQ + 2 * HKV), const),
            pl.BlockSpec((1, MXU_DIM), const),
            pl.BlockSpec((1, MXU_DIM), const),
            pl.BlockSpec((MXU_DIM, MXU_DIM), const),
            tab, tab, tab,
        ],
        out_specs=[
            pl.BlockSpec((None, HQ, TM), lambda b, j: (b, 0, j)),
            pl.BlockSpec((TM, HKV), row),
            pl.BlockSpec((None, HKV, TM), lambda b, j: (b, 0, j)),
        ],
        out_shape=[
            jax.ShapeDtypeStruct((BATCH, HQ, ROWS), BF16),
            jax.ShapeDtypeStruct((n, HKV), BF16),
            jax.ShapeDtypeStruct((BATCH, HKV, ROWS), BF16),
        ],
        compiler_params=_params("arbitrary", "arbitrary"),
        name="qkv_proj",
    )(x, norm_g.reshape(1, D_MODEL), mods, mods, mods, mods, w_qkv.astype(BF16),
      jnp.tile(q_gain, reps).reshape(1, MXU_DIM), jnp.tile(k_gain, reps).reshape(1, MXU_DIM),
      ones_blk, *tables)


SLAB = 64


def _col_reduce(red, x):
    n, c = x.shape
    return red(red(x.reshape(n // SLAB, SLAB, c), axis=0), axis=0, keepdims=True)


def _attend(qt_ref, k_ref, vt_ref, o_ref, ot_s, n_keys):
    k = k_ref[0:n_keys, :]

    def scores(h):
        kv = h // Q_GROUP
        parts = []
        if kv > 0:
            parts.append(jnp.zeros((kv * HEAD_DIM, TQ), BF16))
        parts.append(qt_ref[h * HEAD_DIM:(h + 1) * HEAD_DIM, :])
        if kv < N_KV_HEADS - 1:
            parts.append(jnp.zeros(((N_KV_HEADS - 1 - kv) * HEAD_DIM, TQ), BF16))
        return _dot(k, jnp.concatenate(parts, axis=0))

    st_next = scores(0)
    for h in range(N_HEADS):
        kv = h // Q_GROUP
        st = st_next
        if h + 1 < N_HEADS:
            st_next = scores(h + 1)
        p = jnp.exp2(st - _col_reduce(jnp.max, st))
        l = _col_reduce(jnp.sum, p)
        ot = _dot(vt_ref[kv * HEAD_DIM:(kv + 1) * HEAD_DIM, 0:n_keys], p.astype(BF16))
        ot_s[h * HEAD_DIM:(h + 1) * HEAD_DIM, :] = ot / l
    for c in range(HQ // LANES):
        o_ref[:, c * LANES:(c + 1) * LANES] = ot_s[c * LANES:(c + 1) * LANES, :].T.astype(o_ref.dtype)


def _attn_kernel(first_tile, qt_ref, k_ref, vt_ref, o_ref, ot_s):
    if first_tile > 0:
        _attend(qt_ref, k_ref, vt_ref, o_ref, ot_s, ROWS)
        return
    r = pl.program_id(1)

    @pl.when(r == 0)
    def _():
        _attend(qt_ref, k_ref, vt_ref, o_ref, ot_s, CTX_LEN)

    @pl.when(r != 0)
    def _():
        _attend(qt_ref, k_ref, vt_ref, o_ref, ot_s, ROWS)


def _attention(qt, k, vt, with_ctx):
    first = 0 if with_ctx else 1
    return pl.pallas_call(
        functools.partial(_attn_kernel, first),
        grid=(BATCH, Q_TILES - first),
        in_specs=[
            pl.BlockSpec((None, HQ, TQ), lambda b, r: (b, 0, r + first)),
            pl.BlockSpec((ROWS, HKV), lambda b, r: (b, 0)),
            pl.BlockSpec((None, HKV, ROWS), lambda b, r: (b, 0, 0)),
        ],
        out_specs=pl.BlockSpec((TQ, HQ), lambda b, r: (b * Q_TILES + r + first, 0)),
        out_shape=jax.ShapeDtypeStruct((BATCH * ROWS, HQ), BF16),
        scratch_shapes=[pltpu.VMEM((HQ, TQ), F32)],
        compiler_params=_params("arbitrary", "arbitrary"),
        name="attention",
    )(qt, k, vt)


def _tail_kernel(lat_only, a_ref, x_ref, wm_ref, bm_ref, g1_ref, cg1_ref, ng_ref, sh_ref, csh_ref, sc_ref, csc_ref,
                 g2_ref, cg2_ref, w1_ref, w2_ref, o_ref):
    tm = x_ref.shape[0]
    is_ctx = False if lat_only else _ctx_rows(tm, pl.program_id(1))
    pick = lambda ctx_ref, ref: ref[...] if lat_only else jnp.where(is_ctx, ctx_ref[...], ref[...])
    mix = _dot(a_ref[...], wm_ref[...]) + bm_ref[...]
    xn = x_ref[...] + pick(cg1_ref, g1_ref) * mix
    ms = jnp.mean(xn * xn, axis=-1, keepdims=True)
    h = xn * lax.rsqrt(ms + EPS) * ng_ref[...]
    h = (h * (1.0 + pick(csc_ref, sc_ref)) + pick(csh_ref, sh_ref)).astype(BF16)
    n_chunks = D_FF // TF
    up = lambda c: _dot(h, w1_ref[:, c * TF:(c + 1) * TF])
    u_next = up(0)
    acc = None
    for c in range(n_chunks):
        u = jnp.maximum(u_next, 0.0)
        if c + 1 < n_chunks:
            u_next = up(c + 1)
        part = _dot((u * u).astype(BF16), w2_ref[c * TF:(c + 1) * TF, :])
        acc = part if acc is None else acc + part
    o_ref[...] = xn + pick(cg2_ref, g2_ref) * acc


def _tail(a, x, mods, layer, w_mix, b_mix, norm_g, w1, w2, lat_only=False):
    if lat_only:
        tm = TM_LAT
        tiles = SEQ // tm
        row = pl.BlockSpec((pl.Element(tm), pl.Element(D_MODEL)),
                           lambda b, j: (pl.multiple_of(b * ROWS + CTX_LEN + j * tm, CTX_LEN), 0))
        out_rows = BATCH * SEQ
    else:
        tm = TM
        tiles = TILES_PER_BATCH
        row = pl.BlockSpec((tm, D_MODEL), lambda b, j: (b * tiles + j, 0))
        out_rows = BATCH * ROWS
    once = pl.Buffered(1)
    const = lambda shape: pl.BlockSpec(shape, lambda b, j: (0, 0), pipeline_mode=once)
    return pl.pallas_call(
        functools.partial(_tail_kernel, lat_only),
        grid=(BATCH, tiles),
        in_specs=[
            row,
            row,
            const((D_MODEL, D_MODEL)),
            const((1, D_MODEL)),
            *_mod_specs(layer, 2),
            const((1, D_MODEL)),
            *_mod_specs(layer, 3), *_mod_specs(layer, 4), *_mod_specs(layer, 5),
            const((D_MODEL, D_FF)),
            const((D_FF, D_MODEL)),
        ],
        out_specs=pl.BlockSpec((tm, D_MODEL), lambda b, j: (b * tiles + j, 0)),
        out_shape=jax.ShapeDtypeStruct((out_rows, D_MODEL), F32),
        compiler_params=_params("arbitrary", "arbitrary"),
        name="tail",
    )(a, x, w_mix.astype(BF16), b_mix.reshape(1, D_MODEL), mods, mods, norm_g.reshape(1, D_MODEL),
      mods, mods, mods, mods, mods, mods, w1.astype(BF16), w2.astype(BF16))


def _glu_kernel(x_ref, g_ref, sh_ref, csh_ref, sc_ref, csc_ref, w_ref, b_ref, u_ref):
    j = pl.program_id(1)
    is_ctx = _ctx_rows(TM, j)
    h = _norm_mod(x_ref[...], g_ref[...], is_ctx, sh_ref[...], csh_ref[...], sc_ref[...], csc_ref[...])
    h = h.astype(BF16)

    def proj(c):
        lo, hi = c * MXU_DIM, (c + 1) * MXU_DIM
        return (_dot(h, w_ref[:, lo:hi]) + b_ref[:, lo:hi],
                _dot(h, w_ref[:, D_MODEL + lo:D_MODEL + hi]) + b_ref[:, D_MODEL + lo:D_MODEL + hi])

    nxt = proj(0)
    for c in range(D_MODEL // MXU_DIM):
        val, gate = nxt
        if c + 1 < D_MODEL // MXU_DIM:
            nxt = proj(c + 1)
        u_ref[:, c * MXU_DIM:(c + 1) * MXU_DIM] = val * _sigmoid(gate)


def _gelu_kernel(x_ref, g_ref, sh_ref, csh_ref, sc_ref, csc_ref, w_ref, gate_ref, xr_ref):
    j = pl.program_id(1)
    is_ctx = _ctx_rows(TM, j)
    h = _norm_mod(x_ref[...], g_ref[...], is_ctx, sh_ref[...], csh_ref[...], sc_ref[...], csc_ref[...])
    h = h.astype(BF16)
    proj = lambda lo: _dot(h, w_ref[:, lo:lo + MXU_DIM])
    nxt = proj(0)
    for c in range(D_RNN // MXU_DIM):
        lo = c * MXU_DIM
        pre = nxt
        xr_ref[:, lo:lo + MXU_DIM] = proj(D_RNN + lo)
        if c + 1 < D_RNN // MXU_DIM:
            nxt = proj(lo + MXU_DIM)
        gate_ref[:, lo:lo + MXU_DIM] = jax.nn.gelu(pre)


def _in_proj(kernel, name, x, mods, layer, norm_g, w, bias, n_out):
    row = lambda b, j: (b * TILES_PER_BATCH + j, 0)
    const = lambda b, j: (0, 0)
    extra_specs = [] if bias is None else [pl.BlockSpec((1, 2 * D_MODEL), const)]
    extra_args = [] if bias is None else [bias.reshape(1, 2 * D_MODEL)]
    out_spec = pl.BlockSpec((TM, D_MODEL), row)
    out_shape = jax.ShapeDtypeStruct((BATCH * ROWS, D_MODEL), F32)
    return pl.pallas_call(
        kernel,
        grid=(BATCH, TILES_PER_BATCH),
        in_specs=[
            pl.BlockSpec((TM, D_MODEL), row),
            pl.BlockSpec((1, D_MODEL), const),
            *_mod_specs(layer, 0), *_mod_specs(layer, 1),
            pl.BlockSpec((D_MODEL, 2 * D_MODEL), const),
            *extra_specs,
        ],
        out_specs=out_spec if n_out == 1 else [out_spec] * n_out,
        out_shape=out_shape if n_out == 1 else [out_shape] * n_out,
        compiler_params=_params("arbitrary", "arbitrary"),
        name=name,
    )(x, norm_g.reshape(1, D_MODEL), mods, mods, mods, mods, w.astype(BF16), *extra_args)


CONV_ROWS = 64
SUBLANES = 8
SHIFT_ROWS = TQ + 2 * HALO - SUBLANES


def _conv_kernel(prev_ref, cur_ref, next_ref, w_ref, b_ref, ng_ref, nb_ref, a_ref, buf_s, sh_s, y_s):
    r = pl.program_id(1)
    half = CONV_WIDTH // 2
    prev_ok = (r >= 2).astype(F32)
    next_ok = jnp.logical_and(r >= 1, r <= Q_TILES - 2).astype(F32)
    buf_s[0:HALO, :] = prev_ref[...] * prev_ok
    buf_s[HALO:HALO + TQ, :] = cur_ref[...]
    buf_s[HALO + TQ:HALO + TQ + HALO, :] = next_ref[...] * next_ok
    for s in range(1, SUBLANES):
        sh_s[s - 1] = buf_s[s:s + SHIFT_ROWS, :]
    for c in range(D_MODEL // LANES):
        cols = slice(c * LANES, (c + 1) * LANES)
        w = w_ref[:, cols]
        for rb in range(TQ // CONV_ROWS):
            acc = jnp.zeros((CONV_ROWS, LANES), F32)
            for tap in range(CONV_WIDTH):
                q, s = divmod(HALO - half + tap, SUBLANES)
                rows = slice(q * SUBLANES + rb * CONV_ROWS, q * SUBLANES + (rb + 1) * CONV_ROWS)
                src = buf_s[rows, cols] if s == 0 else sh_s[s - 1, rows, cols]
                acc = acc + w[tap:tap + 1, :] * src
            y_s[rb * CONV_ROWS:(rb + 1) * CONV_ROWS, cols] = acc
    y = y_s[...] + b_ref[...]
    yc = y - jnp.mean(y, axis=-1, keepdims=True)
    var = jnp.mean(yc * yc, axis=-1, keepdims=True)
    z = yc * lax.rsqrt(var + EPS) * ng_ref[...] + nb_ref[...]
    a_ref[...] = (z * _sigmoid(z)).astype(a_ref.dtype)


def _conv_module(u, w_dw, b_dw, norm_g, norm_b):
    per = TQ // HALO
    last = BATCH * ROWS // HALO - 1
    tile = lambda b, r: b * Q_TILES + r
    const = lambda b, r: (0, 0)
    return pl.pallas_call(
        _conv_kernel,
        grid=(BATCH, Q_TILES),
        in_specs=[
            pl.BlockSpec((HALO, D_MODEL), lambda b, r: (jnp.maximum(tile(b, r) * per - 1, 0), 0)),
            pl.BlockSpec((TQ, D_MODEL), lambda b, r: (tile(b, r), 0)),
            pl.BlockSpec((HALO, D_MODEL), lambda b, r: (jnp.minimum((tile(b, r) + 1) * per, last), 0)),
            pl.BlockSpec((CONV_WIDTH, D_MODEL), const),
            pl.BlockSpec((1, D_MODEL), const),
            pl.BlockSpec((1, D_MODEL), const),
            pl.BlockSpec((1, D_MODEL), const),
        ],
        out_specs=pl.BlockSpec((TQ, D_MODEL), lambda b, r: (tile(b, r), 0)),
        out_shape=jax.ShapeDtypeStruct((BATCH * ROWS, D_MODEL), BF16),
        scratch_shapes=[
            pltpu.VMEM((TQ + 2 * HALO, D_MODEL), F32),
            pltpu.VMEM((SUBLANES - 1, SHIFT_ROWS, D_MODEL), F32),
            pltpu.VMEM((TQ, D_MODEL), F32),
        ],
        compiler_params=_params("arbitrary", "arbitrary"),
        name="conv_module",
    )(u, u, u, w_dw, b_dw.reshape(1, D_MODEL), norm_g.reshape(1, D_MODEL), norm_b.reshape(1, D_MODEL))


def _lru_kernel(reverse, x_ref, halo_ref, cw_ref, cb_ref, wr_ref, wi_ref, br_ref, bi_ref, lam_ref, *rest):
    if reverse:
        hf_ref, gate_ref, out_ref, xp_s, a_s, b_s, hs_s, h_s = rest
    else:
        out_ref, xp_s, a_s, b_s, hs_s, h_s = rest
    i = pl.program_id(1)
    chunk = jnp.where(i == 0, 0, N_CHUNKS - i) if reverse else i
    n = BATCH * T_CHUNK
    w = LRU_CONV_WIDTH

    if reverse:
        halo_ok = jnp.logical_and(chunk >= 1, chunk <= N_CHUNKS - 2).astype(F32)
        xp_s[:, 0:T_CHUNK, :] = x_ref[...]
        xp_s[:, T_CHUNK:T_CHUNK + LRU_HALO, :] = halo_ref[...] * halo_ok
        shifted = lambda s: xp_s[:, s:s + T_CHUNK, :]
    else:
        halo_ok = (chunk >= 2).astype(F32)
        xp_s[:, 0:LRU_HALO, :] = halo_ref[...] * halo_ok
        xp_s[:, LRU_HALO:LRU_HALO + T_CHUNK, :] = x_ref[...]
        shifted = lambda s: xp_s[:, LRU_HALO - s:LRU_HALO - s + T_CHUNK, :]
    cw = cw_ref[...]
    u = cb_ref[...] + cw[w - 1:w, :] * shifted(0)
    for s in range(1, w):
        u = u + cw[w - 1 - s:w - s, :] * shifted(s)
    u = u.reshape(n, LRU_BLOCK)

    ub = u.astype(BF16)
    r_gate = 0.5 * jnp.tanh(_dot(ub, wr_ref[...]) + br_ref[...]) + 0.5
    i_gate = 0.5 * jnp.tanh(_dot(ub, wi_ref[...]) + bi_ref[...]) + 0.5
    neg_rate = LRU_C * jax.nn.softplus(-lam_ref[...])
    neg_log_a = r_gate * neg_rate
    a = jnp.exp2(r_gate * (neg_rate * -LOG2E))
    sq = jnp.tanh(neg_log_a) * (a * a + 1.0)
    mult = jnp.where(sq > 0.0, sq * lax.rsqrt(sq), 0.0)
    t_idx = lax.broadcasted_iota(jnp.int32, (n, 1), 0) % T_CHUNK
    start = jnp.logical_and(t_idx == (T_CHUNK - 1 if reverse else 0), i == 0)
    mult = jnp.where(start, 1.0, mult)
    b = mult * i_gate * u
    for c in range(LRU_BLOCK // LANES):
        for bt in range(BATCH):
            rows = slice(bt * T_CHUNK, (bt + 1) * T_CHUNK)
            dst = slice(bt * LRU_PITCH, bt * LRU_PITCH + T_CHUNK)
            a_s[c, dst, :] = a[rows, c * LANES:(c + 1) * LANES]
            b_s[c, dst, :] = b[rows, c * LANES:(c + 1) * LANES]

    @pl.when(i == 0)
    def _():
        h_s[...] = jnp.zeros_like(h_s)

    def step(k, hs):
        t = T_CHUNK - 1 - k if reverse else k
        rows = pl.ds(t, BATCH, stride=LRU_PITCH)
        out = []
        for c, h in enumerate(hs):
            h = a_s[c, rows, :] * h + b_s[c, rows, :]
            hs_s[c, rows, :] = h
            out.append(h)
        return tuple(out)

    h0 = tuple(h_s[c] for c in range(LRU_BLOCK // LANES))
    h1 = lax.fori_loop(0, T_CHUNK, step, h0, unroll=8)
    for c, h in enumerate(h1):
        h_s[c] = h

    for c in range(LRU_BLOCK // LANES):
        cols = slice(c * LANES, (c + 1) * LANES)
        for bt in range(BATCH):
            hs = hs_s[c, bt * LRU_PITCH:bt * LRU_PITCH + T_CHUNK, :]
            if reverse:
                out_ref[bt, :, cols] = ((hf_ref[bt, :, cols] + hs) * gate_ref[bt, :, cols]).astype(out_ref.dtype)
            else:
                out_ref[bt, :, cols] = hs


def _lru_scan(reverse, xr, conv_w, conv_b, gate_w, gate_b, lam, hf=None, gate=None):
    per = T_CHUNK // LRU_HALO
    last = ROWS // LRU_HALO - 1
    if reverse:
        chunk = lambda i: jnp.where(i == 0, 0, N_CHUNKS - i)
        halo = lambda n, i: (0, jnp.minimum((chunk(i) + 1) * per, last), n)
    else:
        chunk = lambda i: i
        halo = lambda n, i: (0, jnp.maximum(chunk(i) * per - 1, 0), n)
    blk = pl.BlockSpec((BATCH, T_CHUNK, LRU_BLOCK), lambda n, i: (0, chunk(i), n))
    vec = pl.BlockSpec((1, LRU_BLOCK), lambda n, i: (0, n))
    mat = pl.BlockSpec((None, LRU_BLOCK, LRU_BLOCK), lambda n, i: (n, 0, 0))
    extra_specs = [blk, blk] if reverse else []
    extra_args = [hf, gate] if reverse else []
    half_w = (0.5 * gate_w).astype(BF16)
    half_b = 0.5 * gate_b
    return pl.pallas_call(
        functools.partial(_lru_kernel, reverse),
        grid=(N_LRU_BLOCKS, N_CHUNKS),
        in_specs=[
            blk,
            pl.BlockSpec((BATCH, LRU_HALO, LRU_BLOCK), halo),
            pl.BlockSpec((LRU_CONV_WIDTH, LRU_BLOCK), lambda n, i: (0, n)),
            vec, mat, mat, vec, vec, vec,
            *extra_specs,
        ],
        out_specs=blk,
        out_shape=jax.ShapeDtypeStruct((BATCH, ROWS, D_RNN), BF16 if reverse else F32),
        scratch_shapes=[
            pltpu.VMEM((BATCH, T_CHUNK + LRU_HALO, LRU_BLOCK), F32),
            pltpu.VMEM((LRU_BLOCK // LANES, BATCH * LRU_PITCH, LANES), F32),
            pltpu.VMEM((LRU_BLOCK // LANES, BATCH * LRU_PITCH, LANES), F32),
            pltpu.VMEM((LRU_BLOCK // LANES, BATCH * LRU_PITCH, LANES), F32),
            pltpu.VMEM((LRU_BLOCK // LANES, BATCH, LANES), F32),
        ],
        compiler_params=_params("arbitrary", "arbitrary"),
        name="lru_bwd" if reverse else "lru_fwd",
    )(xr, xr, conv_w, conv_b.reshape(1, D_RNN), half_w[0], half_w[1],
      half_b[0].reshape(1, D_RNN), half_b[1].reshape(1, D_RNN), lam.reshape(1, D_RNN), *extra_args)


def kernel(x, c, ctx, c_ctx, mod_w, mod_b, norm_mix_g, norm_mlp_g, mlp_w1, mlp_w2, attn_w_qkv, attn_q_gain,
           attn_k_gain, attn_w_o, conv_w_in, conv_b_in, conv_w_dw, conv_b_dw, conv_norm_g, conv_norm_b,
           conv_w_out, conv_b_out, lru_w_in, lru_conv_w, lru_conv_b, lru_gate_w, lru_gate_b, lru_lambda,
           lru_w_out):
    cond = jnp.concatenate([c, c_ctx[None, :], jnp.zeros((COND_ROWS - BATCH - 1, D_MODEL), F32)], axis=0)
    mods = _ada_mod(cond, mod_w, mod_b)
    xs = jnp.concatenate([ctx, x], axis=1).reshape(BATCH * ROWS, D_MODEL)
    tables = _rope_tables()
    no_bias = jnp.zeros((D_MODEL,), F32)
    for i in range(DEPTH):
        kind, j = i % N_MIXERS, i // N_MIXERS
        need_ctx = i < DEPTH - 1
        if kind == 0:
            qt, k, vt = _qkv_proj(xs, mods, i, norm_mix_g[i], attn_w_qkv[j], attn_q_gain[j], attn_k_gain[j], tables)
            a = _attention(qt, k, vt, need_ctx)
            w_mix, b_mix = attn_w_o[j], no_bias
        elif kind == 1:
            u = _in_proj(_glu_kernel, "conv_in", xs, mods, i, norm_mix_g[i], conv_w_in[j], conv_b_in[j], 1)
            a = _conv_module(u, conv_w_dw[j], conv_b_dw[j], conv_norm_g[j], conv_norm_b[j])
            w_mix, b_mix = conv_w_out[j], conv_b_out[j]
        else:
            gate, xr = _in_proj(_gelu_kernel, "lru_in", xs, mods, i, norm_mix_g[i], lru_w_in[j], None, 2)
            gate = gate.reshape(BATCH, ROWS, D_RNN)
            xr = xr.reshape(BATCH, ROWS, D_RNN)
            hf = _lru_scan(False, xr, lru_conv_w[j, 0], lru_conv_b[j, 0], lru_gate_w[j, 0], lru_gate_b[j, 0],
                           lru_lambda[j, 0])
            a = _lru_scan(True, xr, lru_conv_w[j, 1], lru_conv_b[j, 1], lru_gate_w[j, 1], lru_gate_b[j, 1],
                          lru_lambda[j, 1], hf, gate)
            a = a.reshape(BATCH * ROWS, D_RNN)
            w_mix, b_mix = lru_w_out[j], no_bias
        xs = _tail(a, xs, mods, i, w_mix, b_mix, norm_mlp_g[i], mlp_w1[i], mlp_w2[i], lat_only=not need_ctx)
    return xs.reshape(BATCH, SEQ, D_MODEL)
```

```python
import functools

import jax
import jax.numpy as jnp
from jax import lax
from jax.experimental import pallas as pl
from jax.experimental.pallas import tpu as pltpu

D_MODEL = 1024
BATCH = 8
SEQ = 2048
DEPTH = 4
GRID_W = 64
CTX_LEN = 256
N_MIXERS = 3
N_HEADS = 16
N_KV_HEADS = 4
HEAD_DIM = D_MODEL // N_HEADS
Q_GROUP = N_HEADS // N_KV_HEADS
ROPE_FREQS = HEAD_DIM // 4
ROPE_THETA = 10000.0
CONV_WIDTH = 31
D_RNN = D_MODEL
N_LRU_BLOCKS = 4
LRU_BLOCK = D_RNN // N_LRU_BLOCKS
LRU_CONV_WIDTH = 4
LRU_C = 8.0
D_FF = 4 * D_MODEL
EPS = 1e-6
LOG2E = 1.4426950408889634
Q_SCALE = HEAD_DIM ** -0.5 * LOG2E

ROWS = CTX_LEN + SEQ
HQ = N_HEADS * HEAD_DIM
HKV = N_KV_HEADS * HEAD_DIM
COND_ROWS = 16
CTX_COND = BATCH
LANES = 128
MXU_DIM = 256

TM = 768
TILES_PER_BATCH = ROWS // TM
TM_LAT = 512
TQ = 256
Q_TILES = ROWS // TQ
TF = 1024
T_CHUNK = 256
N_CHUNKS = ROWS // T_CHUNK
HALO = 16
LRU_HALO = 8
LRU_PITCH = T_CHUNK + LRU_HALO
VMEM_LIMIT = 56 * 1024 * 1024

F32 = jnp.float32
BF16 = jnp.bfloat16


def _dot(a, b):
    return jnp.dot(a, b, preferred_element_type=F32)


def _sigmoid(x):
    return 0.5 * jnp.tanh(0.5 * x) + 0.5


def _params(*sem):
    return pltpu.CompilerParams(dimension_semantics=sem, vmem_limit_bytes=VMEM_LIMIT)


def _ada_kernel(cond_ref, w_ref, b_ref, out_ref):
    c = cond_ref[...]
    s = (c * _sigmoid(c)).astype(BF16)
    out_ref[...] = _dot(s, w_ref[...].astype(BF16)) + b_ref[...]


def _ada_mod(cond, mod_w, mod_b):
    tn = 1536
    out = pl.pallas_call(
        _ada_kernel,
        grid=(DEPTH, 6 * D_MODEL // tn),
        in_specs=[
            pl.BlockSpec((COND_ROWS, D_MODEL), lambda l, n: (0, 0)),
            pl.BlockSpec((None, D_MODEL, tn), lambda l, n: (l, 0, n)),
            pl.BlockSpec((None, 1, tn), lambda l, n: (l, 0, n)),
        ],
        out_specs=pl.BlockSpec((None, COND_ROWS, tn), lambda l, n: (l, 0, n)),
        out_shape=jax.ShapeDtypeStruct((DEPTH, COND_ROWS, 6 * D_MODEL), F32),
        compiler_params=_params("arbitrary", "arbitrary"),
        name="ada_mod",
    )(cond, mod_w, mod_b.reshape(DEPTH, 1, 6 * D_MODEL))
    return out.reshape(DEPTH, COND_ROWS, 6, 1, D_MODEL)


def _mod_specs(layer, which):
    blk = (None, None, None, 1, D_MODEL)
    return [
        pl.BlockSpec(blk, lambda b, j: (layer, b, which, 0, 0)),
        pl.BlockSpec(blk, lambda b, j: (layer, CTX_COND, which, 0, 0)),
    ]


def _ctx_rows(tm, j):
    rows = lax.broadcasted_iota(jnp.int32, (tm, 1), 0)
    return jnp.logical_and(rows < CTX_LEN, j == 0)


def _stream_specs(x):
    if not isinstance(x, tuple):
        return [pl.BlockSpec((TM, D_MODEL), lambda b, j: (b * TILES_PER_BATCH + j, 0))], [x]
    ctx, lat = x
    lat_row = lambda b, j: pl.multiple_of(b * SEQ + jnp.maximum(j * TM - CTX_LEN, 0), CTX_LEN)
    specs = [pl.BlockSpec((CTX_LEN, D_MODEL), lambda b, j: (b, 0)),
             pl.BlockSpec((pl.Element(TM), pl.Element(D_MODEL)), lambda b, j: (lat_row(b, j), 0))]
    return specs, [ctx.reshape(BATCH * CTX_LEN, D_MODEL), lat.reshape(BATCH * SEQ, D_MODEL)]


def _stream_tile(refs, j):
    if len(refs) == 1:
        return refs[0][...]
    ctx_ref, lat_ref = refs
    lat = lat_ref[...]
    first = jnp.concatenate([ctx_ref[...], lat[0:TM - CTX_LEN]], axis=0)
    return jnp.where(j == 0, first, lat)


def _layer_spec(shape, idx, pipeline_mode=None):
    zeros = (0,) * len(shape)
    return pl.BlockSpec((None, *shape), lambda b, j: (idx, *zeros), pipeline_mode=pipeline_mode)


def _norm_mod(x, g, is_ctx, sh, csh, sc, csc):
    ms = jnp.mean(x * x, axis=-1, keepdims=True)
    h = x * lax.rsqrt(ms + EPS) * g
    scale = jnp.where(is_ctx, csc, sc)
    shift = jnp.where(is_ctx, csh, sh)
    return h * (1.0 + scale) + shift


def _head_norm(z, gain, ones_blk):
    ms = _dot((z * z).astype(BF16), ones_blk)
    return z * lax.rsqrt(ms + EPS) * gain


def _rope(z, c, s_up, s_dn):
    up = pltpu.roll(z, LANES - ROPE_FREQS, axis=1)
    dn = pltpu.roll(z, ROPE_FREQS, axis=1)
    return z * c + up * s_up + dn * s_dn


def _qkv_kernel(n_x, *refs):
    (g_ref, sh_ref, csh_ref, sc_ref, csc_ref, w_ref, qg_ref, kg_ref, ones_ref,
     c_ref, sup_ref, sdn_ref, qt_ref, k_ref, vt_ref) = refs[n_x:]
    j = pl.program_id(1)
    is_ctx = _ctx_rows(TM, j)
    x = _stream_tile(refs[:n_x], j)
    h = _norm_mod(x, g_ref[...], is_ctx, sh_ref[...], csh_ref[...], sc_ref[...], csc_ref[...])
    h = h.astype(BF16)
    ones_blk = ones_ref[...]
    c, s_up, s_dn = c_ref[...], sup_ref[...], sdn_ref[...]
    n_blk = (HQ + 2 * HKV) // MXU_DIM
    proj = lambda blk: _dot(h, w_ref[:, blk * MXU_DIM:(blk + 1) * MXU_DIM])
    z_next = proj(0)
    for blk in range(n_blk):
        z = z_next
        if blk + 1 < n_blk:
            z_next = proj(blk + 1)
        if blk >= (HQ + HKV) // MXU_DIM:
            vt_ref[...] = z.astype(BF16).T
            continue
        is_q = blk < HQ // MXU_DIM
        zn = _head_norm(z, qg_ref[...] if is_q else kg_ref[...], ones_blk)
        for half in range(MXU_DIM // LANES):
            r = _rope(zn[:, half * LANES:(half + 1) * LANES], c, s_up, s_dn)
            if is_q:
                col = blk * MXU_DIM + half * LANES
                qt_ref[col:col + LANES, :] = (r * Q_SCALE).T.astype(BF16)
            else:
                col = half * LANES
                k_ref[:, col:col + LANES] = r.astype(BF16)


def _rope_tables():
    rows = SEQ // GRID_W
    row = jnp.repeat(jnp.arange(rows, dtype=jnp.int32), GRID_W)
    col = jnp.tile(jnp.arange(GRID_W, dtype=jnp.int32), rows)
    pos = jnp.stack([row, col], axis=-1).astype(F32)
    inv = ROPE_THETA ** (-jnp.arange(ROPE_FREQS, dtype=F32) / ROPE_FREQS)
    ang = pos[:, :, None] * inv
    cos, sin = jnp.cos(ang), jnp.sin(ang)
    zero = jnp.zeros_like(sin)
    c = jnp.concatenate([cos, cos], axis=-1).reshape(SEQ, HEAD_DIM)
    s_up = jnp.concatenate([-sin, zero], axis=-1).reshape(SEQ, HEAD_DIM)
    s_dn = jnp.concatenate([zero, sin], axis=-1).reshape(SEQ, HEAD_DIM)

    def full(t, ctx_val):
        t = jnp.tile(t, (1, LANES // HEAD_DIM))
        return jnp.concatenate([jnp.full((CTX_LEN, LANES), ctx_val, F32), t], axis=0)

    return full(c, 1.0), full(s_up, 0.0), full(s_dn, 0.0)


def _qkv_proj(x, mods, layer, norm_g, w_qkv, idx, q_gain, k_gain, tables):
    n = BATCH * ROWS
    ones_blk = (jnp.kron(jnp.eye(MXU_DIM // HEAD_DIM, dtype=F32), jnp.ones((HEAD_DIM, HEAD_DIM), F32))
                / HEAD_DIM).astype(BF16)
    reps = MXU_DIM // HEAD_DIM
    row = lambda b, j: (b * TILES_PER_BATCH + j, 0)
    const = lambda b, j: (0, 0)
    tab = pl.BlockSpec((TM, LANES), lambda b, j: (j, 0))
    x_specs, x_args = _stream_specs(x)
    return pl.pallas_call(
        functools.partial(_qkv_kernel, len(x_args)),
        grid=(BATCH, TILES_PER_BATCH),
        in_specs=[
            *x_specs,
            pl.BlockSpec((1, D_MODEL), const),
            *_mod_specs(layer, 0), *_mod_specs(layer, 1),
            _layer_spec((D_MODEL, HQ + 2 * HKV), idx),
            pl.BlockSpec((1, MXU_DIM), const),
            pl.BlockSpec((1, MXU_DIM), const),
            pl.BlockSpec((MXU_DIM, MXU_DIM), const),
            tab, tab, tab,
        ],
        out_specs=[
            pl.BlockSpec((None, HQ, TM), lambda b, j: (b, 0, j)),
            pl.BlockSpec((TM, HKV), row),
            pl.BlockSpec((None, HKV, TM), lambda b, j: (b, 0, j)),
        ],
        out_shape=[
            jax.ShapeDtypeStruct((BATCH, HQ, ROWS), BF16),
            jax.ShapeDtypeStruct((n, HKV), BF16),
            jax.ShapeDtypeStruct((BATCH, HKV, ROWS), BF16),
        ],
        compiler_params=_params("arbitrary", "arbitrary"),
        name="qkv_proj",
    )(*x_args, norm_g.reshape(1, D_MODEL), mods, mods, mods, mods, w_qkv,
      jnp.tile(q_gain, reps).reshape(1, MXU_DIM), jnp.tile(k_gain, reps).reshape(1, MXU_DIM),
      ones_blk, *tables)


SLAB = 64
KEY_CHUNK = 256
LOOKAHEAD = 6
ONES_ROWS = 16


def _col_reduce(red, x):
    n, c = x.shape
    return red(red(x.reshape(n // SLAB, SLAB, c), axis=0), axis=0, keepdims=True)


def _attend(qt_ref, k_ref, vt_ref, o_ref, ot_s, n_keys):
    n_chunks = n_keys // KEY_CHUNK
    steps = [(h, c) for h in range(N_HEADS) for c in range(n_chunks)]

    def scores(step):
        h, c = step
        kv = h // Q_GROUP
        parts = []
        if kv > 0:
            parts.append(jnp.zeros((kv * HEAD_DIM, TQ), BF16))
        parts.append(qt_ref[h * HEAD_DIM:(h + 1) * HEAD_DIM, :])
        if kv < N_KV_HEADS - 1:
            parts.append(jnp.zeros(((N_KV_HEADS - 1 - kv) * HEAD_DIM, TQ), BF16))
        return _dot(k_ref[c * KEY_CHUNK:(c + 1) * KEY_CHUNK, :], jnp.concatenate(parts, axis=0))

    ones = jnp.ones((ONES_ROWS, KEY_CHUNK), BF16)
    pending = [scores(s) for s in steps[:LOOKAHEAD]]
    m_old = acc_old = None
    for i, (h, c) in enumerate(steps):
        kv = h // Q_GROUP
        st = pending.pop(0)
        if i + LOOKAHEAD < len(steps):
            pending.append(scores(steps[i + LOOKAHEAD]))
        vt = jnp.concatenate([vt_ref[kv * HEAD_DIM:(kv + 1) * HEAD_DIM, c * KEY_CHUNK:(c + 1) * KEY_CHUNK], ones],
                             axis=0)
        m = _col_reduce(jnp.max, st)
        if c == 0:
            acc = _dot(vt, jnp.exp2(st - m).astype(BF16))
        else:
            m = jnp.maximum(m_old, m)
            acc = jnp.exp2(m_old - m) * acc_old + _dot(vt, jnp.exp2(st - m).astype(BF16))
        m_old, acc_old = m, acc
        if c == n_chunks - 1:
            ot_s[h * HEAD_DIM:(h + 1) * HEAD_DIM, :] = acc[0:HEAD_DIM] / acc[HEAD_DIM:HEAD_DIM + 1]
    for c in range(HQ // LANES):
        o_ref[:, c * LANES:(c + 1) * LANES] = ot_s[c * LANES:(c + 1) * LANES, :].T.astype(o_ref.dtype)


def _attn_kernel(first_tile, qt_ref, k_ref, vt_ref, o_ref, ot_s):
    if first_tile > 0:
        _attend(qt_ref, k_ref, vt_ref, o_ref, ot_s, ROWS)
        return
    r = pl.program_id(1)

    @pl.when(r == 0)
    def _():
        _attend(qt_ref, k_ref, vt_ref, o_ref, ot_s, CTX_LEN)

    @pl.when(r != 0)
    def _():
        _attend(qt_ref, k_ref, vt_ref, o_ref, ot_s, ROWS)


def _attention(qt, k, vt, with_ctx):
    first = 0 if with_ctx else 1
    return pl.pallas_call(
        functools.partial(_attn_kernel, first),
        grid=(BATCH, Q_TILES - first),
        in_specs=[
            pl.BlockSpec((None, HQ, TQ), lambda b, r: (b, 0, r + first)),
            pl.BlockSpec((ROWS, HKV), lambda b, r: (b, 0)),
            pl.BlockSpec((None, HKV, ROWS), lambda b, r: (b, 0, 0)),
        ],
        out_specs=pl.BlockSpec((TQ, HQ), lambda b, r: (b * Q_TILES + r + first, 0)),
        out_shape=jax.ShapeDtypeStruct((BATCH * ROWS, HQ), BF16),
        scratch_shapes=[pltpu.VMEM((HQ, TQ), F32)],
        compiler_params=_params("arbitrary", "arbitrary"),
        name="attention",
    )(qt, k, vt)


def _tail_kernel(lat_only, has_bias, n_x, a_ref, *refs):
    x_refs = refs[:n_x]
    refs = list(refs[n_x:])
    wm_ref = refs.pop(0)
    bm_ref = refs.pop(0) if has_bias else None
    g1_ref, cg1_ref, ng_ref, sh_ref, csh_ref, sc_ref, csc_ref, g2_ref, cg2_ref, w1_ref, w2_ref, o_ref = refs
    j = pl.program_id(1)
    is_ctx = False if lat_only else _ctx_rows(o_ref.shape[0], j)
    pick = lambda ctx_ref, ref: ref[...] if lat_only else jnp.where(is_ctx, ctx_ref[...], ref[...])
    mix = _dot(a_ref[...], wm_ref[...])
    if has_bias:
        mix = mix + bm_ref[...]
    xn = _stream_tile(x_refs, j) + pick(cg1_ref, g1_ref) * mix
    ms = jnp.mean(xn * xn, axis=-1, keepdims=True)
    h = xn * lax.rsqrt(ms + EPS) * ng_ref[...]
    h = (h * (1.0 + pick(csc_ref, sc_ref)) + pick(csh_ref, sh_ref)).astype(BF16)
    n_chunks = D_FF // TF
    up = lambda c: _dot(h, w1_ref[:, c * TF:(c + 1) * TF])
    u_next = up(0)
    acc = None
    for c in range(n_chunks):
        u = jnp.maximum(u_next, 0.0)
        if c + 1 < n_chunks:
            u_next = up(c + 1)
        part = _dot((u * u).astype(BF16), w2_ref[c * TF:(c + 1) * TF, :])
        acc = part if acc is None else acc + part
    o_ref[...] = xn + pick(cg2_ref, g2_ref) * acc


def _tail(a, x, mods, layer, w_mix, mix_idx, b_mix, norm_g, w1, w2, lat_only=False):
    if lat_only:
        tm = TM_LAT
        tiles = SEQ // tm
        row = pl.BlockSpec((pl.Element(tm), pl.Element(D_MODEL)),
                           lambda b, j: (pl.multiple_of(b * ROWS + CTX_LEN + j * tm, CTX_LEN), 0))
        x_specs, x_args = [row], [x]
        out_rows = BATCH * SEQ
    else:
        tm = TM
        tiles = TILES_PER_BATCH
        row = pl.BlockSpec((tm, D_MODEL), lambda b, j: (b * tiles + j, 0))
        x_specs, x_args = _stream_specs(x)
        out_rows = BATCH * ROWS
    once = pl.Buffered(1)
    vec = pl.BlockSpec((1, D_MODEL), lambda b, j: (0, 0))
    has_bias = b_mix is not None
    return pl.pallas_call(
        functools.partial(_tail_kernel, lat_only, has_bias, len(x_args)),
        grid=(BATCH, tiles),
        in_specs=[
            row,
            *x_specs,
            _layer_spec((D_MODEL, D_MODEL), mix_idx, once),
            *([vec] if has_bias else []),
            *_mod_specs(layer, 2),
            vec,
            *_mod_specs(layer, 3), *_mod_specs(layer, 4), *_mod_specs(layer, 5),
            _layer_spec((D_MODEL, D_FF), layer, once),
            _layer_spec((D_FF, D_MODEL), layer, once),
        ],
        out_specs=pl.BlockSpec((tm, D_MODEL), lambda b, j: (b * tiles + j, 0)),
        out_shape=jax.ShapeDtypeStruct((out_rows, D_MODEL), F32),
        compiler_params=_params("arbitrary", "arbitrary"),
        name="tail",
    )(a, *x_args, w_mix, *([b_mix.reshape(1, D_MODEL)] if has_bias else []), mods, mods,
      norm_g.reshape(1, D_MODEL), mods, mods, mods, mods, mods, mods, w1, w2)


def _glu_kernel(x_ref, g_ref, sh_ref, csh_ref, sc_ref, csc_ref, w_ref, b_ref, u_ref):
    j = pl.program_id(1)
    is_ctx = _ctx_rows(TM, j)
    h = _norm_mod(x_ref[...], g_ref[...], is_ctx, sh_ref[...], csh_ref[...], sc_ref[...], csc_ref[...])
    h = h.astype(BF16)

    def proj(c):
        lo, hi = c * MXU_DIM, (c + 1) * MXU_DIM
        return (_dot(h, w_ref[:, lo:hi]) + b_ref[:, lo:hi],
                _dot(h, w_ref[:, D_MODEL + lo:D_MODEL + hi]) + b_ref[:, D_MODEL + lo:D_MODEL + hi])

    nxt = proj(0)
    for c in range(D_MODEL // MXU_DIM):
        val, gate = nxt
        if c + 1 < D_MODEL // MXU_DIM:
            nxt = proj(c + 1)
        u_ref[:, c * MXU_DIM:(c + 1) * MXU_DIM] = val * _sigmoid(gate)


def _gelu_kernel(x_ref, g_ref, sh_ref, csh_ref, sc_ref, csc_ref, w_ref, gate_ref, xr_ref):
    j = pl.program_id(1)
    is_ctx = _ctx_rows(TM, j)
    h = _norm_mod(x_ref[...], g_ref[...], is_ctx, sh_ref[...], csh_ref[...], sc_ref[...], csc_ref[...])
    h = h.astype(BF16)
    proj = lambda lo: _dot(h, w_ref[:, lo:lo + MXU_DIM])
    nxt = proj(0)
    for c in range(D_RNN // MXU_DIM):
        lo = c * MXU_DIM
        pre = nxt
        xr_ref[:, lo:lo + MXU_DIM] = proj(D_RNN + lo)
        if c + 1 < D_RNN // MXU_DIM:
            nxt = proj(lo + MXU_DIM)
        gate_ref[:, lo:lo + MXU_DIM] = jax.nn.gelu(pre)


def _in_proj(kernel, name, x, mods, layer, norm_g, w, bias, n_out):
    row = lambda b, j: (b * TILES_PER_BATCH + j, 0)
    const = lambda b, j: (0, 0)
    extra_specs = [] if bias is None else [pl.BlockSpec((1, 2 * D_MODEL), const)]
    extra_args = [] if bias is None else [bias.reshape(1, 2 * D_MODEL)]
    out_spec = pl.BlockSpec((TM, D_MODEL), row)
    out_shape = jax.ShapeDtypeStruct((BATCH * ROWS, D_MODEL), F32)
    return pl.pallas_call(
        kernel,
        grid=(BATCH, TILES_PER_BATCH),
        in_specs=[
            pl.BlockSpec((TM, D_MODEL), row),
            pl.BlockSpec((1, D_MODEL), const),
            *_mod_specs(layer, 0), *_mod_specs(layer, 1),
            pl.BlockSpec((D_MODEL, 2 * D_MODEL), const),
            *extra_specs,
        ],
        out_specs=out_spec if n_out == 1 else [out_spec] * n_out,
        out_shape=out_shape if n_out == 1 else [out_shape] * n_out,
        compiler_params=_params("arbitrary", "arbitrary"),
        name=name,
    )(x, norm_g.reshape(1, D_MODEL), mods, mods, mods, mods, w.astype(BF16), *extra_args)


CONV_ROWS = 64
SUBLANES = 8
SHIFT_ROWS = TQ + 2 * HALO - SUBLANES


def _conv_kernel(prev_ref, cur_ref, next_ref, w_ref, b_ref, ng_ref, nb_ref, a_ref, buf_s, sh_s, y_s):
    r = pl.program_id(1)
    half = CONV_WIDTH // 2
    prev_ok = (r >= 2).astype(F32)
    next_ok = jnp.logical_and(r >= 1, r <= Q_TILES - 2).astype(F32)
    buf_s[0:HALO, :] = prev_ref[...] * prev_ok
    buf_s[HALO:HALO + TQ, :] = cur_ref[...]
    buf_s[HALO + TQ:HALO + TQ + HALO, :] = next_ref[...] * next_ok
    for s in range(1, SUBLANES):
        sh_s[s - 1] = buf_s[s:s + SHIFT_ROWS, :]
    for c in range(D_MODEL // LANES):
        cols = slice(c * LANES, (c + 1) * LANES)
        w = w_ref[:, cols]
        def row_block(rb, carry):
            base = pl.multiple_of(rb * CONV_ROWS, CONV_ROWS)
            acc = jnp.zeros((CONV_ROWS, LANES), F32)
            for s in range(SUBLANES):
                qs = [q for q in range(-(-(HALO + half + 1) // SUBLANES))
                      if 0 <= q * SUBLANES + s - (HALO - half) < CONV_WIDTH]
                rows = pl.ds(base + qs[0] * SUBLANES, (qs[-1] - qs[0]) * SUBLANES + CONV_ROWS)
                strip = buf_s[rows, cols] if s == 0 else sh_s[s - 1, rows, cols]
                for q in qs:
                    tap = q * SUBLANES + s - (HALO - half)
                    off = (q - qs[0]) * SUBLANES
                    acc = acc + w[tap:tap + 1, :] * strip[off:off + CONV_ROWS]
            y_s[pl.ds(base, CONV_ROWS), cols] = acc
            return carry

        lax.fori_loop(0, TQ // CONV_ROWS, row_block, 0)
    y = y_s[...] + b_ref[...]
    yc = y - jnp.mean(y, axis=-1, keepdims=True)
    var = jnp.mean(yc * yc, axis=-1, keepdims=True)
    z = yc * lax.rsqrt(var + EPS) * ng_ref[...] + nb_ref[...]
    a_ref[...] = (z * _sigmoid(z)).astype(a_ref.dtype)


def _conv_module(u, w_dw, b_dw, norm_g, norm_b):
    per = TQ // HALO
    last = BATCH * ROWS // HALO - 1
    tile = lambda b, r: b * Q_TILES + r
    const = lambda b, r: (0, 0)
    return pl.pallas_call(
        _conv_kernel,
        grid=(BATCH, Q_TILES),
        in_specs=[
            pl.BlockSpec((HALO, D_MODEL), lambda b, r: (jnp.maximum(tile(b, r) * per - 1, 0), 0)),
            pl.BlockSpec((TQ, D_MODEL), lambda b, r: (tile(b, r), 0)),
            pl.BlockSpec((HALO, D_MODEL), lambda b, r: (jnp.minimum((tile(b, r) + 1) * per, last), 0)),
            pl.BlockSpec((CONV_WIDTH, D_MODEL), const),
            pl.BlockSpec((1, D_MODEL), const),
            pl.BlockSpec((1, D_MODEL), const),
            pl.BlockSpec((1, D_MODEL), const),
        ],
        out_specs=pl.BlockSpec((TQ, D_MODEL), lambda b, r: (tile(b, r), 0)),
        out_shape=jax.ShapeDtypeStruct((BATCH * ROWS, D_MODEL), BF16),
        scratch_shapes=[
            pltpu.VMEM((TQ + 2 * HALO, D_MODEL), F32),
            pltpu.VMEM((SUBLANES - 1, SHIFT_ROWS, D_MODEL), F32),
            pltpu.VMEM((TQ, D_MODEL), F32),
        ],
        compiler_params=_params("arbitrary", "arbitrary"),
        name="conv_module",
    )(u, u, u, w_dw, b_dw.reshape(1, D_MODEL), norm_g.reshape(1, D_MODEL), norm_b.reshape(1, D_MODEL))


def _lru_kernel(reverse, x_ref, halo_ref, cw_ref, cb_ref, wr_ref, wi_ref, br_ref, bi_ref, lam_ref, *rest):
    if reverse:
        hf_ref, gate_ref, out_ref, xp_s, a_s, b_s, hs_s, h_s = rest
    else:
        out_ref, xp_s, a_s, b_s, hs_s, h_s = rest
    i = pl.program_id(1)
    chunk = jnp.where(i == 0, 0, N_CHUNKS - i) if reverse else i
    n = BATCH * T_CHUNK
    w = LRU_CONV_WIDTH

    if reverse:
        halo_ok = jnp.logical_and(chunk >= 1, chunk <= N_CHUNKS - 2).astype(F32)
        xp_s[:, 0:T_CHUNK, :] = x_ref[...]
        xp_s[:, T_CHUNK:T_CHUNK + LRU_HALO, :] = halo_ref[...] * halo_ok
        shifted = lambda s: xp_s[:, s:s + T_CHUNK, :]
    else:
        halo_ok = (chunk >= 2).astype(F32)
        xp_s[:, 0:LRU_HALO, :] = halo_ref[...] * halo_ok
        xp_s[:, LRU_HALO:LRU_HALO + T_CHUNK, :] = x_ref[...]
        shifted = lambda s: xp_s[:, LRU_HALO - s:LRU_HALO - s + T_CHUNK, :]
    cw = cw_ref[...]
    u = cb_ref[...] + cw[w - 1:w, :] * shifted(0)
    for s in range(1, w):
        u = u + cw[w - 1 - s:w - s, :] * shifted(s)
    u = u.reshape(n, LRU_BLOCK)

    ub = u.astype(BF16)
    r_gate = 0.5 * jnp.tanh(_dot(ub, wr_ref[...]) + br_ref[...]) + 0.5
    i_gate = 0.5 * jnp.tanh(_dot(ub, wi_ref[...]) + bi_ref[...]) + 0.5
    neg_rate = LRU_C * jax.nn.softplus(-lam_ref[...])
    neg_log_a = r_gate * neg_rate
    a = jnp.exp2(r_gate * (neg_rate * -LOG2E))
    sq = jnp.tanh(neg_log_a) * (a * a + 1.0)
    mult = jnp.where(sq > 0.0, sq * lax.rsqrt(sq), 0.0)
    t_idx = lax.broadcasted_iota(jnp.int32, (n, 1), 0) % T_CHUNK
    start = jnp.logical_and(t_idx == (T_CHUNK - 1 if reverse else 0), i == 0)
    mult = jnp.where(start, 1.0, mult)
    b = mult * i_gate * u
    for c in range(LRU_BLOCK // LANES):
        for bt in range(BATCH):
            rows = slice(bt * T_CHUNK, (bt + 1) * T_CHUNK)
            dst = slice(bt * LRU_PITCH, bt * LRU_PITCH + T_CHUNK)
            a_s[c, dst, :] = a[rows, c * LANES:(c + 1) * LANES]
            b_s[c, dst, :] = b[rows, c * LANES:(c + 1) * LANES]

    @pl.when(i == 0)
    def _():
        h_s[...] = jnp.zeros_like(h_s)

    def step(k, hs):
        t = T_CHUNK - 1 - k if reverse else k
        rows = pl.ds(t, BATCH, stride=LRU_PITCH)
        out = []
        for c, h in enumerate(hs):
            h = a_s[c, rows, :] * h + b_s[c, rows, :]
            hs_s[c, rows, :] = h
            out.append(h)
        return tuple(out)

    h0 = tuple(h_s[c] for c in range(LRU_BLOCK // LANES))
    h1 = lax.fori_loop(0, T_CHUNK, step, h0, unroll=8)
    for c, h in enumerate(h1):
        h_s[c] = h

    for c in range(LRU_BLOCK // LANES):
        cols = slice(c * LANES, (c + 1) * LANES)
        for bt in range(BATCH):
            hs = hs_s[c, bt * LRU_PITCH:bt * LRU_PITCH + T_CHUNK, :]
            if reverse:
                out_ref[bt, :, cols] = ((hf_ref[bt, :, cols] + hs) * gate_ref[bt, :, cols]).astype(out_ref.dtype)
            else:
                out_ref[bt, :, cols] = hs


def _lru_scan(reverse, xr, conv_w, conv_b, gate_w, gate_b, lam, hf=None, gate=None):
    per = T_CHUNK // LRU_HALO
    last = ROWS // LRU_HALO - 1
    if reverse:
        chunk = lambda i: jnp.where(i == 0, 0, N_CHUNKS - i)
        halo = lambda n, i: (0, jnp.minimum((chunk(i) + 1) * per, last), n)
    else:
        chunk = lambda i: i
        halo = lambda n, i: (0, jnp.maximum(chunk(i) * per - 1, 0), n)
    blk = pl.BlockSpec((BATCH, T_CHUNK, LRU_BLOCK), lambda n, i: (0, chunk(i), n))
    vec = pl.BlockSpec((1, LRU_BLOCK), lambda n, i: (0, n))
    mat = pl.BlockSpec((None, LRU_BLOCK, LRU_BLOCK), lambda n, i: (n, 0, 0))
    extra_specs = [blk, blk] if reverse else []
    extra_args = [hf, gate] if reverse else []
    half_w = (0.5 * gate_w).astype(BF16)
    half_b = 0.5 * gate_b
    return pl.pallas_call(
        functools.partial(_lru_kernel, reverse),
        grid=(N_LRU_BLOCKS, N_CHUNKS),
        in_specs=[
            blk,
            pl.BlockSpec((BATCH, LRU_HALO, LRU_BLOCK), halo),
            pl.BlockSpec((LRU_CONV_WIDTH, LRU_BLOCK), lambda n, i: (0, n)),
            vec, mat, mat, vec, vec, vec,
            *extra_specs,
        ],
        out_specs=blk,
        out_shape=jax.ShapeDtypeStruct((BATCH, ROWS, D_RNN), BF16 if reverse else F32),
        scratch_shapes=[
            pltpu.VMEM((BATCH, T_CHUNK + LRU_HALO, LRU_BLOCK), F32),
            pltpu.VMEM((LRU_BLOCK // LANES, BATCH * LRU_PITCH, LANES), F32),
            pltpu.VMEM((LRU_BLOCK // LANES, BATCH * LRU_PITCH, LANES), F32),
            pltpu.VMEM((LRU_BLOCK // LANES, BATCH * LRU_PITCH, LANES), F32),
            pltpu.VMEM((LRU_BLOCK // LANES, BATCH, LANES), F32),
        ],
        compiler_params=_params("arbitrary", "arbitrary"),
        name="lru_bwd" if reverse else "lru_fwd",
    )(xr, xr, conv_w, conv_b.reshape(1, D_RNN), half_w[0], half_w[1],
      half_b[0].reshape(1, D_RNN), half_b[1].reshape(1, D_RNN), lam.reshape(1, D_RNN), *extra_args)


def kernel(x, c, ctx, c_ctx, mod_w, mod_b, norm_mix_g, norm_mlp_g, mlp_w1, mlp_w2, attn_w_qkv, attn_q_gain,
           attn_k_gain, attn_w_o, conv_w_in, conv_b_in, conv_w_dw, conv_b_dw, conv_norm_g, conv_norm_b,
           conv_w_out, conv_b_out, lru_w_in, lru_conv_w, lru_conv_b, lru_gate_w, lru_gate_b, lru_lambda,
           lru_w_out):
    cond = jnp.concatenate([c, c_ctx[None, :], jnp.zeros((COND_ROWS - BATCH - 1, D_MODEL), F32)], axis=0)
    mods = _ada_mod(cond, mod_w, mod_b)
    xs = (ctx, x)
    tables = _rope_tables()
    w_qkv, w_o = attn_w_qkv.astype(BF16), attn_w_o.astype(BF16)
    w_conv_out, w_lru_out = conv_w_out.astype(BF16), lru_w_out.astype(BF16)
    w1, w2 = mlp_w1.astype(BF16), mlp_w2.astype(BF16)
    for i in range(DEPTH):
        kind, j = i % N_MIXERS, i // N_MIXERS
        need_ctx = i < DEPTH - 1
        if kind == 0:
            qt, k, vt = _qkv_proj(xs, mods, i, norm_mix_g[i], w_qkv, j, attn_q_gain[j], attn_k_gain[j], tables)
            a = _attention(qt, k, vt, need_ctx)
            w_mix, b_mix = w_o, None
        elif kind == 1:
            u = _in_proj(_glu_kernel, "conv_in", xs, mods, i, norm_mix_g[i], conv_w_in[j], conv_b_in[j], 1)
            a = _conv_module(u, conv_w_dw[j], conv_b_dw[j], conv_norm_g[j], conv_norm_b[j])
            w_mix, b_mix = w_conv_out, conv_b_out[j]
        else:
            gate, xr = _in_proj(_gelu_kernel, "lru_in", xs, mods, i, norm_mix_g[i], lru_w_in[j], None, 2)
            gate = gate.reshape(BATCH, ROWS, D_RNN)
            xr = xr.reshape(BATCH, ROWS, D_RNN)
            hf = _lru_scan(False, xr, lru_conv_w[j, 0], lru_conv_b[j, 0], lru_gate_w[j, 0], lru_gate_b[j, 0],
                           lru_lambda[j, 0])
            a = _lru_scan(True, xr, lru_conv_w[j, 1], lru_conv_b[j, 1], lru_gate_w[j, 1], lru_gate_b[j, 1],
                          lru_lambda[j, 1], hf, gate)
            a = a.reshape(BATCH * ROWS, D_RNN)
            w_mix, b_mix = w_lru_out, None
        xs = _tail(a, xs, mods, i, w_mix, j, b_mix, norm_mlp_g[i], w1, w2, lat_only=not need_ctx)
    return xs.reshape(BATCH, SEQ, D_MODEL)
```

```python
import functools

import jax
import jax.numpy as jnp
from jax import lax
from jax.experimental import pallas as pl
from jax.experimental.pallas import tpu as pltpu

D_MODEL = 1024
BATCH = 8
SEQ = 2048
DEPTH = 4
GRID_W = 64
CTX_LEN = 256
N_MIXERS = 3
N_HEADS = 16
N_KV_HEADS = 4
HEAD_DIM = D_MODEL // N_HEADS
Q_GROUP = N_HEADS // N_KV_HEADS
ROPE_FREQS = HEAD_DIM // 4
ROPE_THETA = 10000.0
CONV_WIDTH = 31
D_RNN = D_MODEL
N_LRU_BLOCKS = 4
LRU_BLOCK = D_RNN // N_LRU_BLOCKS
LRU_CONV_WIDTH = 4
LRU_C = 8.0
D_FF = 4 * D_MODEL
EPS = 1e-6
LOG2E = 1.4426950408889634
Q_SCALE = HEAD_DIM ** -0.5 * LOG2E

ROWS = CTX_LEN + SEQ
HQ = N_HEADS * HEAD_DIM
HKV = N_KV_HEADS * HEAD_DIM
COND_ROWS = 16
CTX_COND = BATCH
LANES = 128
MXU_DIM = 256

TM = 768
TILES_PER_BATCH = ROWS // TM
TM_LAT = 512
TQ = 256
Q_TILES = ROWS // TQ
TF = 1024
T_CHUNK = 256
N_CHUNKS = ROWS // T_CHUNK
HALO = 16
LRU_HALO = 8
LRU_PITCH = T_CHUNK + LRU_HALO
VMEM_LIMIT = 56 * 1024 * 1024

F32 = jnp.float32
BF16 = jnp.bfloat16


def _dot(a, b):
    return jnp.dot(a, b, preferred_element_type=F32)


def _sigmoid(x):
    return 0.5 * jnp.tanh(0.5 * x) + 0.5


def _params(*sem):
    return pltpu.CompilerParams(dimension_semantics=sem, vmem_limit_bytes=VMEM_LIMIT)


def _ada_kernel(cond_ref, w_ref, b_ref, out_ref):
    c = cond_ref[...]
    s = (c * _sigmoid(c)).astype(BF16)
    out_ref[...] = _dot(s, w_ref[...].astype(BF16)) + b_ref[...]


def _ada_mod(cond, mod_w, mod_b):
    tn = 1536
    out = pl.pallas_call(
        _ada_kernel,
        grid=(DEPTH, 6 * D_MODEL // tn),
        in_specs=[
            pl.BlockSpec((COND_ROWS, D_MODEL), lambda l, n: (0, 0)),
            pl.BlockSpec((None, D_MODEL, tn), lambda l, n: (l, 0, n)),
            pl.BlockSpec((None, 1, tn), lambda l, n: (l, 0, n)),
        ],
        out_specs=pl.BlockSpec((None, COND_ROWS, tn), lambda l, n: (l, 0, n)),
        out_shape=jax.ShapeDtypeStruct((DEPTH, COND_ROWS, 6 * D_MODEL), F32),
        compiler_params=_params("arbitrary", "arbitrary"),
        name="ada_mod",
    )(cond, mod_w, mod_b.reshape(DEPTH, 1, 6 * D_MODEL))
    return out.reshape(DEPTH, COND_ROWS, 6, 1, D_MODEL)


def _mod_specs(layer, which):
    blk = (None, None, None, 1, D_MODEL)
    return [
        pl.BlockSpec(blk, lambda b, j: (layer, b, which, 0, 0)),
        pl.BlockSpec(blk, lambda b, j: (layer, CTX_COND, which, 0, 0)),
    ]


def _ctx_rows(tm, j):
    rows = lax.broadcasted_iota(jnp.int32, (tm, 1), 0)
    return jnp.logical_and(rows < CTX_LEN, j == 0)


def _stream_specs(x):
    if not isinstance(x, tuple):
        return [pl.BlockSpec((TM, D_MODEL), lambda b, j: (b * TILES_PER_BATCH + j, 0))], [x]
    ctx, lat = x
    lat_row = lambda b, j: pl.multiple_of(b * SEQ + jnp.maximum(j * TM - CTX_LEN, 0), CTX_LEN)
    specs = [pl.BlockSpec((CTX_LEN, D_MODEL), lambda b, j: (b, 0)),
             pl.BlockSpec((pl.Element(TM), pl.Element(D_MODEL)), lambda b, j: (lat_row(b, j), 0))]
    return specs, [ctx.reshape(BATCH * CTX_LEN, D_MODEL), lat.reshape(BATCH * SEQ, D_MODEL)]


def _stream_tile(refs, j):
    if len(refs) == 1:
        return refs[0][...]
    ctx_ref, lat_ref = refs
    lat = lat_ref[...]
    first = jnp.concatenate([ctx_ref[...], lat[0:TM - CTX_LEN]], axis=0)
    return jnp.where(j == 0, first, lat)


def _layer_spec(shape, idx, pipeline_mode=None):
    zeros = (0,) * len(shape)
    return pl.BlockSpec((None, *shape), lambda b, j: (idx, *zeros), pipeline_mode=pipeline_mode)


def _norm_mod(x, g, is_ctx, sh, csh, sc, csc):
    ms = jnp.mean(x * x, axis=-1, keepdims=True)
    h = x * lax.rsqrt(ms + EPS) * g
    scale = jnp.where(is_ctx, csc, sc)
    shift = jnp.where(is_ctx, csh, sh)
    return h * (1.0 + scale) + shift


def _head_norm(z, gain, ones_blk):
    ms = _dot((z * z).astype(BF16), ones_blk)
    return z * lax.rsqrt(ms + EPS) * gain


def _rope(z, c, s_up, s_dn):
    up = pltpu.roll(z, LANES - ROPE_FREQS, axis=1)
    dn = pltpu.roll(z, ROPE_FREQS, axis=1)
    return z * c + up * s_up + dn * s_dn


def _qkv_kernel(n_x, *refs):
    (g_ref, sh_ref, csh_ref, sc_ref, csc_ref, w_ref, qg_ref, kg_ref, ones_ref,
     c_ref, sup_ref, sdn_ref, qt_ref, k_ref, vt_ref) = refs[n_x:]
    j = pl.program_id(1)
    is_ctx = _ctx_rows(TM, j)
    x = _stream_tile(refs[:n_x], j)
    h = _norm_mod(x, g_ref[...], is_ctx, sh_ref[...], csh_ref[...], sc_ref[...], csc_ref[...])
    h = h.astype(BF16)
    ones_blk = ones_ref[...]
    c, s_up, s_dn = c_ref[...], sup_ref[...], sdn_ref[...]
    n_blk = (HQ + 2 * HKV) // MXU_DIM
    proj = lambda blk: _dot(h, w_ref[:, blk * MXU_DIM:(blk + 1) * MXU_DIM])
    z_next = proj(0)
    for blk in range(n_blk):
        z = z_next
        if blk + 1 < n_blk:
            z_next = proj(blk + 1)
        if blk >= (HQ + HKV) // MXU_DIM:
            vt_ref[...] = z.astype(BF16).T
            continue
        is_q = blk < HQ // MXU_DIM
        zn = _head_norm(z, qg_ref[...] if is_q else kg_ref[...], ones_blk)
        for half in range(MXU_DIM // LANES):
            r = _rope(zn[:, half * LANES:(half + 1) * LANES], c, s_up, s_dn)
            if is_q:
                col = blk * MXU_DIM + half * LANES
                qt_ref[col:col + LANES, :] = (r * Q_SCALE).T.astype(BF16)
            else:
                col = half * LANES
                k_ref[:, col:col + LANES] = r.astype(BF16)


def _rope_tables():
    rows = SEQ // GRID_W
    row = jnp.repeat(jnp.arange(rows, dtype=jnp.int32), GRID_W)
    col = jnp.tile(jnp.arange(GRID_W, dtype=jnp.int32), rows)
    pos = jnp.stack([row, col], axis=-1).astype(F32)
    inv = ROPE_THETA ** (-jnp.arange(ROPE_FREQS, dtype=F32) / ROPE_FREQS)
    ang = pos[:, :, None] * inv
    cos, sin = jnp.cos(ang), jnp.sin(ang)
    zero = jnp.zeros_like(sin)
    c = jnp.concatenate([cos, cos], axis=-1).reshape(SEQ, HEAD_DIM)
    s_up = jnp.concatenate([-sin, zero], axis=-1).reshape(SEQ, HEAD_DIM)
    s_dn = jnp.concatenate([zero, sin], axis=-1).reshape(SEQ, HEAD_DIM)

    def full(t, ctx_val):
        t = jnp.tile(t, (1, LANES // HEAD_DIM))
        return jnp.concatenate([jnp.full((CTX_LEN, LANES), ctx_val, F32), t], axis=0)

    return full(c, 1.0), full(s_up, 0.0), full(s_dn, 0.0)


def _qkv_proj(x, mods, layer, norm_g, w_qkv, idx, q_gain, k_gain, tables):
    n = BATCH * ROWS
    ones_blk = (jnp.kron(jnp.eye(MXU_DIM // HEAD_DIM, dtype=F32), jnp.ones((HEAD_DIM, HEAD_DIM), F32))
                / HEAD_DIM).astype(BF16)
    reps = MXU_DIM // HEAD_DIM
    row = lambda b, j: (b * TILES_PER_BATCH + j, 0)
    const = lambda b, j: (0, 0)
    tab = pl.BlockSpec((TM, LANES), lambda b, j: (j, 0))
    x_specs, x_args = _stream_specs(x)
    return pl.pallas_call(
        functools.partial(_qkv_kernel, len(x_args)),
        grid=(BATCH, TILES_PER_BATCH),
        in_specs=[
            *x_specs,
            pl.BlockSpec((1, D_MODEL), const),
            *_mod_specs(layer, 0), *_mod_specs(layer, 1),
            _layer_spec((D_MODEL, HQ + 2 * HKV), idx),
            pl.BlockSpec((1, MXU_DIM), const),
            pl.BlockSpec((1, MXU_DIM), const),
            pl.BlockSpec((MXU_DIM, MXU_DIM), const),
            tab, tab, tab,
        ],
        out_specs=[
            pl.BlockSpec((None, HQ, TM), lambda b, j: (b, 0, j)),
            pl.BlockSpec((TM, HKV), row),
            pl.BlockSpec((None, HKV, TM), lambda b, j: (b, 0, j)),
        ],
        out_shape=[
            jax.ShapeDtypeStruct((BATCH, HQ, ROWS), BF16),
            jax.ShapeDtypeStruct((n, HKV), BF16),
            jax.ShapeDtypeStruct((BATCH, HKV, ROWS), BF16),
        ],
        compiler_params=_params("arbitrary", "arbitrary"),
        name="qkv_proj",
    )(*x_args, norm_g.reshape(1, D_MODEL), mods, mods, mods, mods, w_qkv,
      jnp.tile(q_gain, reps).reshape(1, MXU_DIM), jnp.tile(k_gain, reps).reshape(1, MXU_DIM),
      ones_blk, *tables)


SLAB = 64
KEY_CHUNK = 256
LOOKAHEAD = 6
ONES_ROWS = 16


def _col_reduce(red, x):
    n, c = x.shape
    return red(red(x.reshape(n // SLAB, SLAB, c), axis=0), axis=0, keepdims=True)


def _attend(qt_ref, k_ref, vt_ref, o_ref, ot_s, n_keys):
    n_chunks = n_keys // KEY_CHUNK
    steps = [(h, c) for h in range(N_HEADS) for c in range(n_chunks)]

    def scores(step):
        h, c = step
        kv = h // Q_GROUP
        per = LANES // HEAD_DIM
        q = qt_ref[h * HEAD_DIM:(h + 1) * HEAD_DIM, :]
        parts = [q if i == kv % per else jnp.zeros((HEAD_DIM, TQ), BF16) for i in range(per)]
        lanes = slice(kv // per * LANES, (kv // per + 1) * LANES)
        return _dot(k_ref[c * KEY_CHUNK:(c + 1) * KEY_CHUNK, lanes], jnp.concatenate(parts, axis=0))

    ones = jnp.ones((ONES_ROWS, KEY_CHUNK), BF16)
    pending = [scores(s) for s in steps[:LOOKAHEAD]]
    m_old = acc_old = None
    for i, (h, c) in enumerate(steps):
        kv = h // Q_GROUP
        st = pending.pop(0)
        if i + LOOKAHEAD < len(steps):
            pending.append(scores(steps[i + LOOKAHEAD]))
        vt = jnp.concatenate([vt_ref[kv * HEAD_DIM:(kv + 1) * HEAD_DIM, c * KEY_CHUNK:(c + 1) * KEY_CHUNK], ones],
                             axis=0)
        m = _col_reduce(jnp.max, st)
        if c == 0:
            acc = _dot(vt, jnp.exp2(st - m).astype(BF16))
        else:
            m = jnp.maximum(m_old, m)
            acc = jnp.exp2(m_old - m) * acc_old + _dot(vt, jnp.exp2(st - m).astype(BF16))
        m_old, acc_old = m, acc
        if c == n_chunks - 1:
            ot_s[h * HEAD_DIM:(h + 1) * HEAD_DIM, :] = acc[0:HEAD_DIM] / acc[HEAD_DIM:HEAD_DIM + 1]
    for c in range(HQ // LANES):
        o_ref[:, c * LANES:(c + 1) * LANES] = ot_s[c * LANES:(c + 1) * LANES, :].T.astype(o_ref.dtype)


def _attn_kernel(first_tile, qt_ref, k_ref, vt_ref, o_ref, ot_s):
    if first_tile > 0:
        _attend(qt_ref, k_ref, vt_ref, o_ref, ot_s, ROWS)
        return
    r = pl.program_id(1)

    @pl.when(r == 0)
    def _():
        _attend(qt_ref, k_ref, vt_ref, o_ref, ot_s, CTX_LEN)

    @pl.when(r != 0)
    def _():
        _attend(qt_ref, k_ref, vt_ref, o_ref, ot_s, ROWS)


def _attention(qt, k, vt, with_ctx):
    first = 0 if with_ctx else 1
    tiles = Q_TILES - first
    return pl.pallas_call(
        functools.partial(_attn_kernel, first),
        grid=(BATCH, tiles),
        in_specs=[
            pl.BlockSpec((None, HQ, TQ), lambda b, r: (b, 0, r + first)),
            pl.BlockSpec((ROWS, HKV), lambda b, r: (b, 0)),
            pl.BlockSpec((None, HKV, ROWS), lambda b, r: (b, 0, 0)),
        ],
        out_specs=pl.BlockSpec((TQ, HQ), lambda b, r: (b * tiles + r, 0)),
        out_shape=jax.ShapeDtypeStruct((BATCH * tiles * TQ, HQ), BF16),
        scratch_shapes=[pltpu.VMEM((HQ, TQ), F32)],
        compiler_params=_params("arbitrary", "arbitrary"),
        name="attention",
    )(qt, k, vt)


def _tail_kernel(lat_only, has_bias, n_x, a_ref, *refs):
    x_refs = refs[:n_x]
    refs = list(refs[n_x:])
    wm_ref = refs.pop(0)
    bm_ref = refs.pop(0) if has_bias else None
    g1_ref, cg1_ref, ng_ref, sh_ref, csh_ref, sc_ref, csc_ref, g2_ref, cg2_ref, w1_ref, w2_ref, o_ref = refs
    j = pl.program_id(1)
    is_ctx = False if lat_only else _ctx_rows(o_ref.shape[0], j)
    pick = lambda ctx_ref, ref: ref[...] if lat_only else jnp.where(is_ctx, ctx_ref[...], ref[...])
    mix = _dot(a_ref[...], wm_ref[...])
    if has_bias:
        mix = mix + bm_ref[...]
    xn = _stream_tile(x_refs, j) + pick(cg1_ref, g1_ref) * mix
    ms = jnp.mean(xn * xn, axis=-1, keepdims=True)
    h = xn * lax.rsqrt(ms + EPS) * ng_ref[...]
    h = (h * (1.0 + pick(csc_ref, sc_ref)) + pick(csh_ref, sh_ref)).astype(BF16)
    n_chunks = D_FF // TF
    up = lambda c: _dot(h, w1_ref[:, c * TF:(c + 1) * TF])
    u_next = up(0)
    acc = None
    for c in range(n_chunks):
        u = jnp.maximum(u_next, 0.0)
        if c + 1 < n_chunks:
            u_next = up(c + 1)
        part = _dot((u * u).astype(BF16), w2_ref[c * TF:(c + 1) * TF, :])
        acc = part if acc is None else acc + part
    o_ref[...] = xn + pick(cg2_ref, g2_ref) * acc


def _tail(a, x, mods, layer, w_mix, mix_idx, b_mix, norm_g, w1, w2, lat_only=False):
    if lat_only:
        tm = TM_LAT
        tiles = SEQ // tm
        row = pl.BlockSpec((pl.Element(tm), pl.Element(D_MODEL)),
                           lambda b, j: (pl.multiple_of(b * ROWS + CTX_LEN + j * tm, CTX_LEN), 0))
        x_specs, x_args = [row], [x]
        out_rows = BATCH * SEQ
        if a.shape[0] == out_rows:
            row = pl.BlockSpec((tm, D_MODEL), lambda b, j: (b * tiles + j, 0))
    else:
        tm = TM
        tiles = TILES_PER_BATCH
        row = pl.BlockSpec((tm, D_MODEL), lambda b, j: (b * tiles + j, 0))
        x_specs, x_args = _stream_specs(x)
        out_rows = BATCH * ROWS
    once = pl.Buffered(1)
    vec = pl.BlockSpec((1, D_MODEL), lambda b, j: (0, 0))
    has_bias = b_mix is not None
    return pl.pallas_call(
        functools.partial(_tail_kernel, lat_only, has_bias, len(x_args)),
        grid=(BATCH, tiles),
        in_specs=[
            row,
            *x_specs,
            _layer_spec((D_MODEL, D_MODEL), mix_idx, once),
            *([vec] if has_bias else []),
            *_mod_specs(layer, 2),
            vec,
            *_mod_specs(layer, 3), *_mod_specs(layer, 4), *_mod_specs(layer, 5),
            _layer_spec((D_MODEL, D_FF), layer, once),
            _layer_spec((D_FF, D_MODEL), layer, once),
        ],
        out_specs=pl.BlockSpec((tm, D_MODEL), lambda b, j: (b * tiles + j, 0)),
        out_shape=jax.ShapeDtypeStruct((out_rows, D_MODEL), F32),
        compiler_params=_params("arbitrary", "arbitrary"),
        name="tail",
    )(a, *x_args, w_mix, *([b_mix.reshape(1, D_MODEL)] if has_bias else []), mods, mods,
      norm_g.reshape(1, D_MODEL), mods, mods, mods, mods, mods, mods, w1, w2)


def _glu_kernel(x_ref, g_ref, sh_ref, csh_ref, sc_ref, csc_ref, w_ref, b_ref, u_ref):
    j = pl.program_id(1)
    is_ctx = _ctx_rows(TM, j)
    h = _norm_mod(x_ref[...], g_ref[...], is_ctx, sh_ref[...], csh_ref[...], sc_ref[...], csc_ref[...])
    h = h.astype(BF16)

    def proj(c):
        lo, hi = c * MXU_DIM, (c + 1) * MXU_DIM
        return (_dot(h, w_ref[:, lo:hi]) + b_ref[:, lo:hi],
                _dot(h, w_ref[:, D_MODEL + lo:D_MODEL + hi]) + b_ref[:, D_MODEL + lo:D_MODEL + hi])

    nxt = proj(0)
    for c in range(D_MODEL // MXU_DIM):
        val, gate = nxt
        if c + 1 < D_MODEL // MXU_DIM:
            nxt = proj(c + 1)
        u_ref[:, c * MXU_DIM:(c + 1) * MXU_DIM] = val * _sigmoid(gate)


def _gelu_kernel(x_ref, g_ref, sh_ref, csh_ref, sc_ref, csc_ref, w_ref, gate_ref, xr_ref):
    j = pl.program_id(1)
    is_ctx = _ctx_rows(TM, j)
    h = _norm_mod(x_ref[...], g_ref[...], is_ctx, sh_ref[...], csh_ref[...], sc_ref[...], csc_ref[...])
    h = h.astype(BF16)
    proj = lambda lo: _dot(h, w_ref[:, lo:lo + MXU_DIM])
    nxt = proj(0)
    for c in range(D_RNN // MXU_DIM):
        lo = c * MXU_DIM
        pre = nxt
        xr_ref[:, lo:lo + MXU_DIM] = proj(D_RNN + lo)
        if c + 1 < D_RNN // MXU_DIM:
            nxt = proj(lo + MXU_DIM)
        gate_ref[:, lo:lo + MXU_DIM] = jax.nn.gelu(pre)


def _in_proj(kernel, name, x, mods, layer, norm_g, w, bias, n_out):
    row = lambda b, j: (b * TILES_PER_BATCH + j, 0)
    const = lambda b, j: (0, 0)
    extra_specs = [] if bias is None else [pl.BlockSpec((1, 2 * D_MODEL), const)]
    extra_args = [] if bias is None else [bias.reshape(1, 2 * D_MODEL)]
    out_spec = pl.BlockSpec((TM, D_MODEL), row)
    out_shape = jax.ShapeDtypeStruct((BATCH * ROWS, D_MODEL), F32)
    return pl.pallas_call(
        kernel,
        grid=(BATCH, TILES_PER_BATCH),
        in_specs=[
            pl.BlockSpec((TM, D_MODEL), row),
            pl.BlockSpec((1, D_MODEL), const),
            *_mod_specs(layer, 0), *_mod_specs(layer, 1),
            pl.BlockSpec((D_MODEL, 2 * D_MODEL), const),
            *extra_specs,
        ],
        out_specs=out_spec if n_out == 1 else [out_spec] * n_out,
        out_shape=out_shape if n_out == 1 else [out_shape] * n_out,
        compiler_params=_params("arbitrary", "arbitrary"),
        name=name,
    )(x, norm_g.reshape(1, D_MODEL), mods, mods, mods, mods, w.astype(BF16), *extra_args)


CONV_ROWS = 64
SUBLANES = 8
SHIFT_ROWS = TQ + 2 * HALO - SUBLANES


def _conv_kernel(prev_ref, cur_ref, next_ref, w_ref, b_ref, ng_ref, nb_ref, a_ref, buf_s, sh_s, y_s):
    r = pl.program_id(1)
    half = CONV_WIDTH // 2
    prev_ok = (r >= 2).astype(F32)
    next_ok = jnp.logical_and(r >= 1, r <= Q_TILES - 2).astype(F32)
    buf_s[0:HALO, :] = prev_ref[...] * prev_ok
    buf_s[HALO:HALO + TQ, :] = cur_ref[...]
    buf_s[HALO + TQ:HALO + TQ + HALO, :] = next_ref[...] * next_ok
    for s in range(1, SUBLANES):
        sh_s[s - 1] = buf_s[s:s + SHIFT_ROWS, :]
    for c in range(D_MODEL // LANES):
        cols = slice(c * LANES, (c + 1) * LANES)
        w = w_ref[:, cols]
        def row_block(rb, carry):
            base = pl.multiple_of(rb * CONV_ROWS, CONV_ROWS)
            acc = jnp.zeros((CONV_ROWS, LANES), F32)
            for s in range(SUBLANES):
                qs = [q for q in range(-(-(HALO + half + 1) // SUBLANES))
                      if 0 <= q * SUBLANES + s - (HALO - half) < CONV_WIDTH]
                rows = pl.ds(base + qs[0] * SUBLANES, (qs[-1] - qs[0]) * SUBLANES + CONV_ROWS)
                strip = buf_s[rows, cols] if s == 0 else sh_s[s - 1, rows, cols]
                for q in qs:
                    tap = q * SUBLANES + s - (HALO - half)
                    off = (q - qs[0]) * SUBLANES
                    acc = acc + w[tap:tap + 1, :] * strip[off:off + CONV_ROWS]
            y_s[pl.ds(base, CONV_ROWS), cols] = acc
            return carry

        lax.fori_loop(0, TQ // CONV_ROWS, row_block, 0)
    y = y_s[...] + b_ref[...]
    yc = y - jnp.mean(y, axis=-1, keepdims=True)
    var = jnp.mean(yc * yc, axis=-1, keepdims=True)
    z = yc * lax.rsqrt(var + EPS) * ng_ref[...] + nb_ref[...]
    a_ref[...] = (z * _sigmoid(z)).astype(a_ref.dtype)


def _conv_module(u, w_dw, b_dw, norm_g, norm_b):
    per = TQ // HALO
    last = BATCH * ROWS // HALO - 1
    tile = lambda b, r: b * Q_TILES + r
    const = lambda b, r: (0, 0)
    return pl.pallas_call(
        _conv_kernel,
        grid=(BATCH, Q_TILES),
        in_specs=[
            pl.BlockSpec((HALO, D_MODEL), lambda b, r: (jnp.maximum(tile(b, r) * per - 1, 0), 0)),
            pl.BlockSpec((TQ, D_MODEL), lambda b, r: (tile(b, r), 0)),
            pl.BlockSpec((HALO, D_MODEL), lambda b, r: (jnp.minimum((tile(b, r) + 1) * per, last), 0)),
            pl.BlockSpec((CONV_WIDTH, D_MODEL), const),
            pl.BlockSpec((1, D_MODEL), const),
            pl.BlockSpec((1, D_MODEL), const),
            pl.BlockSpec((1, D_MODEL), const),
        ],
        out_specs=pl.BlockSpec((TQ, D_MODEL), lambda b, r: (tile(b, r), 0)),
        out_shape=jax.ShapeDtypeStruct((BATCH * ROWS, D_MODEL), BF16),
        scratch_shapes=[
            pltpu.VMEM((TQ + 2 * HALO, D_MODEL), F32),
            pltpu.VMEM((SUBLANES - 1, SHIFT_ROWS, D_MODEL), F32),
            pltpu.VMEM((TQ, D_MODEL), F32),
        ],
        compiler_params=_params("arbitrary", "arbitrary"),
        name="conv_module",
    )(u, u, u, w_dw, b_dw.reshape(1, D_MODEL), norm_g.reshape(1, D_MODEL), norm_b.reshape(1, D_MODEL))


def _lru_kernel(reverse, x_ref, halo_ref, cw_ref, cb_ref, wr_ref, wi_ref, br_ref, bi_ref, lam_ref, *rest):
    if reverse:
        hf_ref, gate_ref, out_ref, xp_s, a_s, b_s, hs_s, h_s = rest
    else:
        out_ref, xp_s, a_s, b_s, hs_s, h_s = rest
    i = pl.program_id(1)
    chunk = jnp.where(i == 0, 0, N_CHUNKS - i) if reverse else i
    n = BATCH * T_CHUNK
    w = LRU_CONV_WIDTH

    if reverse:
        halo_ok = jnp.logical_and(chunk >= 1, chunk <= N_CHUNKS - 2).astype(F32)
        xp_s[:, 0:T_CHUNK, :] = x_ref[...]
        xp_s[:, T_CHUNK:T_CHUNK + LRU_HALO, :] = halo_ref[...] * halo_ok
        shifted = lambda s: xp_s[:, s:s + T_CHUNK, :]
    else:
        halo_ok = (chunk >= 2).astype(F32)
        xp_s[:, 0:LRU_HALO, :] = halo_ref[...] * halo_ok
        xp_s[:, LRU_HALO:LRU_HALO + T_CHUNK, :] = x_ref[...]
        shifted = lambda s: xp_s[:, LRU_HALO - s:LRU_HALO - s + T_CHUNK, :]
    cw = cw_ref[...]
    u = cb_ref[...] + cw[w - 1:w, :] * shifted(0)
    for s in range(1, w):
        u = u + cw[w - 1 - s:w - s, :] * shifted(s)
    u = u.reshape(n, LRU_BLOCK)

    ub = u.astype(BF16)
    r_gate = 0.5 * jnp.tanh(_dot(ub, wr_ref[...]) + br_ref[...]) + 0.5
    i_gate = 0.5 * jnp.tanh(_dot(ub, wi_ref[...]) + bi_ref[...]) + 0.5
    neg_rate = LRU_C * jax.nn.softplus(-lam_ref[...])
    neg_log_a = r_gate * neg_rate
    a = jnp.exp2(r_gate * (neg_rate * -LOG2E))
    sq = jnp.tanh(neg_log_a) * (a * a + 1.0)
    mult = jnp.where(sq > 0.0, sq * lax.rsqrt(sq), 0.0)
    t_idx = lax.broadcasted_iota(jnp.int32, (n, 1), 0) % T_CHUNK
    start = jnp.logical_and(t_idx == (T_CHUNK - 1 if reverse else 0), i == 0)
    mult = jnp.where(start, 1.0, mult)
    b = mult * i_gate * u
    for c in range(LRU_BLOCK // LANES):
        for bt in range(BATCH):
            rows = slice(bt * T_CHUNK, (bt + 1) * T_CHUNK)
            dst = slice(bt * LRU_PITCH, bt * LRU_PITCH + T_CHUNK)
            a_s[c, dst, :] = a[rows, c * LANES:(c + 1) * LANES]
            b_s[c, dst, :] = b[rows, c * LANES:(c + 1) * LANES]

    @pl.when(i == 0)
    def _():
        h_s[...] = jnp.zeros_like(h_s)

    def step(k, hs):
        t = T_CHUNK - 1 - k if reverse else k
        rows = pl.ds(t, BATCH, stride=LRU_PITCH)
        out = []
        for c, h in enumerate(hs):
            h = a_s[c, rows, :] * h + b_s[c, rows, :]
            hs_s[c, rows, :] = h
            out.append(h)
        return tuple(out)

    h0 = tuple(h_s[c] for c in range(LRU_BLOCK // LANES))
    h1 = lax.fori_loop(0, T_CHUNK, step, h0, unroll=8)
    for c, h in enumerate(h1):
        h_s[c] = h

    for c in range(LRU_BLOCK // LANES):
        cols = slice(c * LANES, (c + 1) * LANES)
        for bt in range(BATCH):
            hs = hs_s[c, bt * LRU_PITCH:bt * LRU_PITCH + T_CHUNK, :]
            if reverse:
                out_ref[bt, :, cols] = ((hf_ref[bt, :, cols] + hs) * gate_ref[bt, :, cols]).astype(out_ref.dtype)
            else:
                out_ref[bt, :, cols] = hs


def _lru_scan(reverse, xr, conv_w, conv_b, gate_w, gate_b, lam, hf=None, gate=None):
    per = T_CHUNK // LRU_HALO
    last = ROWS // LRU_HALO - 1
    if reverse:
        chunk = lambda i: jnp.where(i == 0, 0, N_CHUNKS - i)
        halo = lambda n, i: (0, jnp.minimum((chunk(i) + 1) * per, last), n)
    else:
        chunk = lambda i: i
        halo = lambda n, i: (0, jnp.maximum(chunk(i) * per - 1, 0), n)
    blk = pl.BlockSpec((BATCH, T_CHUNK, LRU_BLOCK), lambda n, i: (0, chunk(i), n))
    vec = pl.BlockSpec((1, LRU_BLOCK), lambda n, i: (0, n))
    mat = pl.BlockSpec((None, LRU_BLOCK, LRU_BLOCK), lambda n, i: (n, 0, 0))
    extra_specs = [blk, blk] if reverse else []
    extra_args = [hf, gate] if reverse else []
    half_w = (0.5 * gate_w).astype(BF16)
    half_b = 0.5 * gate_b
    return pl.pallas_call(
        functools.partial(_lru_kernel, reverse),
        grid=(N_LRU_BLOCKS, N_CHUNKS),
        in_specs=[
            blk,
            pl.BlockSpec((BATCH, LRU_HALO, LRU_BLOCK), halo),
            pl.BlockSpec((LRU_CONV_WIDTH, LRU_BLOCK), lambda n, i: (0, n)),
            vec, mat, mat, vec, vec, vec,
            *extra_specs,
        ],
        out_specs=blk,
        out_shape=jax.ShapeDtypeStruct((BATCH, ROWS, D_RNN), BF16 if reverse else F32),
        scratch_shapes=[
            pltpu.VMEM((BATCH, T_CHUNK + LRU_HALO, LRU_BLOCK), F32),
            pltpu.VMEM((LRU_BLOCK // LANES, BATCH * LRU_PITCH, LANES), F32),
            pltpu.VMEM((LRU_BLOCK // LANES, BATCH * LRU_PITCH, LANES), F32),
            pltpu.VMEM((LRU_BLOCK // LANES, BATCH * LRU_PITCH, LANES), F32),
            pltpu.VMEM((LRU_BLOCK // LANES, BATCH, LANES), F32),
        ],
        compiler_params=_params("arbitrary", "arbitrary"),
        name="lru_bwd" if reverse else "lru_fwd",
    )(xr, xr, conv_w, conv_b.reshape(1, D_RNN), half_w[0], half_w[1],
      half_b[0].reshape(1, D_RNN), half_b[1].reshape(1, D_RNN), lam.reshape(1, D_RNN), *extra_args)


def kernel(x, c, ctx, c_ctx, mod_w, mod_b, norm_mix_g, norm_mlp_g, mlp_w1, mlp_w2, attn_w_qkv, attn_q_gain,
           attn_k_gain, attn_w_o, conv_w_in, conv_b_in, conv_w_dw, conv_b_dw, conv_norm_g, conv_norm_b,
           conv_w_out, conv_b_out, lru_w_in, lru_conv_w, lru_conv_b, lru_gate_w, lru_gate_b, lru_lambda,
           lru_w_out):
    cond = jnp.concatenate([c, c_ctx[None, :], jnp.zeros((COND_ROWS - BATCH - 1, D_MODEL), F32)], axis=0)
    mods = _ada_mod(cond, mod_w, mod_b)
    xs = (ctx, x)
    tables = _rope_tables()
    w_qkv, w_o = attn_w_qkv.astype(BF16), attn_w_o.astype(BF16)
    w_conv_out, w_lru_out = conv_w_out.astype(BF16), lru_w_out.astype(BF16)
    w1, w2 = mlp_w1.astype(BF16), mlp_w2.astype(BF16)
    for i in range(DEPTH):
        kind, j = i % N_MIXERS, i // N_MIXERS
        need_ctx = i < DEPTH - 1
        if kind == 0:
            qt, k, vt = _qkv_proj(xs, mods, i, norm_mix_g[i], w_qkv, j, attn_q_gain[j], attn_k_gain[j], tables)
            a = _attention(qt, k, vt, need_ctx)
            w_mix, b_mix = w_o, None
        elif kind == 1:
            u = _in_proj(_glu_kernel, "conv_in", xs, mods, i, norm_mix_g[i], conv_w_in[j], conv_b_in[j], 1)
            a = _conv_module(u, conv_w_dw[j], conv_b_dw[j], conv_norm_g[j], conv_norm_b[j])
            w_mix, b_mix = w_conv_out, conv_b_out[j]
        else:
            gate, xr = _in_proj(_gelu_kernel, "lru_in", xs, mods, i, norm_mix_g[i], lru_w_in[j], None, 2)
            gate = gate.reshape(BATCH, ROWS, D_RNN)
            xr = xr.reshape(BATCH, ROWS, D_RNN)
            hf = _lru_scan(False, xr, lru_conv_w[j, 0], lru_conv_b[j, 0], lru_gate_w[j, 0], lru_gate_b[j, 0],
                           lru_lambda[j, 0])
            a = _lru_scan(True, xr, lru_conv_w[j, 1], lru_conv_b[j, 1], lru_gate_w[j, 1], lru_gate_b[j, 1],
                          lru_lambda[j, 1], hf, gate)
            a = a.reshape(BATCH * ROWS, D_RNN)
            w_mix, b_mix = w_lru_out, None
        xs = _tail(a, xs, mods, i, w_mix, j, b_mix, norm_mlp_g[i], w1, w2, lat_only=not need_ctx)
    return xs.reshape(BATCH, SEQ, D_MODEL)
```

```python
import functools

import jax
import jax.numpy as jnp
from jax import lax
from jax.experimental import pallas as pl
from jax.experimental.pallas import tpu as pltpu

D_MODEL = 1024
BATCH = 8
SEQ = 2048
DEPTH = 4
GRID_W = 64
CTX_LEN = 256
N_MIXERS = 3
N_HEADS = 16
N_KV_HEADS = 4
HEAD_DIM = D_MODEL // N_HEADS
Q_GROUP = N_HEADS // N_KV_HEADS
ROPE_FREQS = HEAD_DIM // 4
ROPE_THETA = 10000.0
CONV_WIDTH = 31
D_RNN = D_MODEL
N_LRU_BLOCKS = 4
LRU_BLOCK = D_RNN // N_LRU_BLOCKS
LRU_CONV_WIDTH = 4
LRU_C = 8.0
D_FF = 4 * D_MODEL
EPS = 1e-6
LOG2E = 1.4426950408889634
Q_SCALE = HEAD_DIM ** -0.5 * LOG2E

ROWS = CTX_LEN + SEQ
HQ = N_HEADS * HEAD_DIM
HKV = N_KV_HEADS * HEAD_DIM
COND_ROWS = 16
CTX_COND = BATCH
LANES = 128
MXU_DIM = 256

TM = 768
TILES_PER_BATCH = ROWS // TM
TM_LAT = 512
TQ = 256
Q_TILES = ROWS // TQ
TF = 1024
T_CHUNK = 256
N_CHUNKS = ROWS // T_CHUNK
HALO = 16
LRU_HALO = 8
LRU_PITCH = T_CHUNK + LRU_HALO
VMEM_LIMIT = 56 * 1024 * 1024

F32 = jnp.float32
BF16 = jnp.bfloat16


def _dot(a, b):
    return jnp.dot(a, b, preferred_element_type=F32)


def _sigmoid(x):
    return 0.5 * jnp.tanh(0.5 * x) + 0.5


def _params(*sem):
    return pltpu.CompilerParams(dimension_semantics=sem, vmem_limit_bytes=VMEM_LIMIT)


def _ada_kernel(cond_ref, w_ref, b_ref, out_ref):
    c = cond_ref[...]
    s = (c * _sigmoid(c)).astype(BF16)
    out_ref[...] = _dot(s, w_ref[...].astype(BF16)) + b_ref[...]


def _ada_mod(cond, mod_w, mod_b):
    tn = 1536
    out = pl.pallas_call(
        _ada_kernel,
        grid=(DEPTH, 6 * D_MODEL // tn),
        in_specs=[
            pl.BlockSpec((COND_ROWS, D_MODEL), lambda l, n: (0, 0)),
            pl.BlockSpec((None, D_MODEL, tn), lambda l, n: (l, 0, n)),
            pl.BlockSpec((None, 1, tn), lambda l, n: (l, 0, n)),
        ],
        out_specs=pl.BlockSpec((None, COND_ROWS, tn), lambda l, n: (l, 0, n)),
        out_shape=jax.ShapeDtypeStruct((DEPTH, COND_ROWS, 6 * D_MODEL), F32),
        compiler_params=_params("arbitrary", "arbitrary"),
        name="ada_mod",
    )(cond, mod_w, mod_b.reshape(DEPTH, 1, 6 * D_MODEL))
    return out.reshape(DEPTH, COND_ROWS, 6, 1, D_MODEL)


def _mod_specs(layer, which):
    blk = (None, None, None, 1, D_MODEL)
    return [
        pl.BlockSpec(blk, lambda b, j: (layer, b, which, 0, 0)),
        pl.BlockSpec(blk, lambda b, j: (layer, CTX_COND, which, 0, 0)),
    ]


def _ctx_rows(tm, j):
    rows = lax.broadcasted_iota(jnp.int32, (tm, 1), 0)
    return jnp.logical_and(rows < CTX_LEN, j == 0)


def _stream_specs(x):
    if not isinstance(x, tuple):
        return [pl.BlockSpec((TM, D_MODEL), lambda b, j: (b * TILES_PER_BATCH + j, 0))], [x]
    ctx, lat = x
    lat_row = lambda b, j: pl.multiple_of(b * SEQ + jnp.maximum(j * TM - CTX_LEN, 0), CTX_LEN)
    specs = [pl.BlockSpec((CTX_LEN, D_MODEL), lambda b, j: (b, 0)),
             pl.BlockSpec((pl.Element(TM), pl.Element(D_MODEL)), lambda b, j: (lat_row(b, j), 0))]
    return specs, [ctx.reshape(BATCH * CTX_LEN, D_MODEL), lat.reshape(BATCH * SEQ, D_MODEL)]


def _stream_tile(refs, j):
    if len(refs) == 1:
        return refs[0][...]
    ctx_ref, lat_ref = refs
    lat = lat_ref[...]
    first = jnp.concatenate([ctx_ref[...], lat[0:TM - CTX_LEN]], axis=0)
    return jnp.where(j == 0, first, lat)


def _layer_spec(shape, idx, pipeline_mode=None):
    zeros = (0,) * len(shape)
    return pl.BlockSpec((None, *shape), lambda b, j: (idx, *zeros), pipeline_mode=pipeline_mode)


def _norm_mod(x, g, is_ctx, sh, csh, sc, csc):
    ms = jnp.mean(x * x, axis=-1, keepdims=True)
    h = x * lax.rsqrt(ms + EPS) * g
    scale = jnp.where(is_ctx, csc, sc)
    shift = jnp.where(is_ctx, csh, sh)
    return h * (1.0 + scale) + shift


def _head_norm(z, gain, ones_blk):
    ms = _dot((z * z).astype(BF16), ones_blk)
    return z * lax.rsqrt(ms + EPS) * gain


def _rope(z, c, s_up, s_dn):
    up = pltpu.roll(z, LANES - ROPE_FREQS, axis=1)
    dn = pltpu.roll(z, ROPE_FREQS, axis=1)
    return z * c + up * s_up + dn * s_dn


def _qkv_kernel(n_x, *refs):
    (g_ref, sh_ref, csh_ref, sc_ref, csc_ref, w_ref, qg_ref, kg_ref, ones_ref,
     c_ref, sup_ref, sdn_ref, qt_ref, k_ref, vt_ref) = refs[n_x:]
    j = pl.program_id(1)
    is_ctx = _ctx_rows(TM, j)
    x = _stream_tile(refs[:n_x], j)
    h = _norm_mod(x, g_ref[...], is_ctx, sh_ref[...], csh_ref[...], sc_ref[...], csc_ref[...])
    h = h.astype(BF16)
    ones_blk = ones_ref[...]
    c, s_up, s_dn = c_ref[...], sup_ref[...], sdn_ref[...]
    n_blk = (HQ + 2 * HKV) // MXU_DIM
    proj = lambda blk: _dot(h, w_ref[:, blk * MXU_DIM:(blk + 1) * MXU_DIM])
    z_next = proj(0)
    for blk in range(n_blk):
        z = z_next
        if blk + 1 < n_blk:
            z_next = proj(blk + 1)
        if blk >= (HQ + HKV) // MXU_DIM:
            vt_ref[...] = z.astype(BF16).T
            continue
        is_q = blk < HQ // MXU_DIM
        zn = _head_norm(z, qg_ref[...] if is_q else kg_ref[...], ones_blk)
        for half in range(MXU_DIM // LANES):
            r = _rope(zn[:, half * LANES:(half + 1) * LANES], c, s_up, s_dn)
            if is_q:
                col = blk * MXU_DIM + half * LANES
                qt_ref[col:col + LANES, :] = (r * Q_SCALE).T.astype(BF16)
            else:
                col = half * LANES
                k_ref[:, col:col + LANES] = r.astype(BF16)


def _rope_tables():
    rows = SEQ // GRID_W
    row = jnp.repeat(jnp.arange(rows, dtype=jnp.int32), GRID_W)
    col = jnp.tile(jnp.arange(GRID_W, dtype=jnp.int32), rows)
    pos = jnp.stack([row, col], axis=-1).astype(F32)
    inv = ROPE_THETA ** (-jnp.arange(ROPE_FREQS, dtype=F32) / ROPE_FREQS)
    ang = pos[:, :, None] * inv
    cos, sin = jnp.cos(ang), jnp.sin(ang)
    zero = jnp.zeros_like(sin)
    c = jnp.concatenate([cos, cos], axis=-1).reshape(SEQ, HEAD_DIM)
    s_up = jnp.concatenate([-sin, zero], axis=-1).reshape(SEQ, HEAD_DIM)
    s_dn = jnp.concatenate([zero, sin], axis=-1).reshape(SEQ, HEAD_DIM)

    def full(t, ctx_val):
        t = jnp.tile(t, (1, LANES // HEAD_DIM))
        return jnp.concatenate([jnp.full((CTX_LEN, LANES), ctx_val, F32), t], axis=0)

    return full(c, 1.0), full(s_up, 0.0), full(s_dn, 0.0)


def _qkv_proj(x, mods, layer, norm_g, w_qkv, idx, q_gain, k_gain, tables):
    n = BATCH * ROWS
    ones_blk = (jnp.kron(jnp.eye(MXU_DIM // HEAD_DIM, dtype=F32), jnp.ones((HEAD_DIM, HEAD_DIM), F32))
                / HEAD_DIM).astype(BF16)
    reps = MXU_DIM // HEAD_DIM
    row = lambda b, j: (b * TILES_PER_BATCH + j, 0)
    const = lambda b, j: (0, 0)
    tab = pl.BlockSpec((TM, LANES), lambda b, j: (j, 0))
    x_specs, x_args = _stream_specs(x)
    return pl.pallas_call(
        functools.partial(_qkv_kernel, len(x_args)),
        grid=(BATCH, TILES_PER_BATCH),
        in_specs=[
            *x_specs,
            pl.BlockSpec((1, D_MODEL), const),
            *_mod_specs(layer, 0), *_mod_specs(layer, 1),
            _layer_spec((D_MODEL, HQ + 2 * HKV), idx),
            pl.BlockSpec((1, MXU_DIM), const),
            pl.BlockSpec((1, MXU_DIM), const),
            pl.BlockSpec((MXU_DIM, MXU_DIM), const),
            tab, tab, tab,
        ],
        out_specs=[
            pl.BlockSpec((None, HQ, TM), lambda b, j: (b, 0, j)),
            pl.BlockSpec((TM, HKV), row),
            pl.BlockSpec((None, HKV, TM), lambda b, j: (b, 0, j)),
        ],
        out_shape=[
            jax.ShapeDtypeStruct((BATCH, HQ, ROWS), BF16),
            jax.ShapeDtypeStruct((n, HKV), BF16),
            jax.ShapeDtypeStruct((BATCH, HKV, ROWS), BF16),
        ],
        compiler_params=_params("arbitrary", "arbitrary"),
        name="qkv_proj",
    )(*x_args, norm_g.reshape(1, D_MODEL), mods, mods, mods, mods, w_qkv,
      jnp.tile(q_gain, reps).reshape(1, MXU_DIM), jnp.tile(k_gain, reps).reshape(1, MXU_DIM),
      ones_blk, *tables)


SLAB = 64
KEY_CHUNK = 256
LOOKAHEAD = 6
ONES_ROWS = 16


def _col_reduce(red, x):
    n, c = x.shape
    return red(red(x.reshape(n // SLAB, SLAB, c), axis=0), axis=0, keepdims=True)


def _attend(qt_ref, k_ref, vt_ref, o_ref, ot_s, n_keys):
    n_chunks = n_keys // KEY_CHUNK
    steps = [(h, c) for h in range(N_HEADS) for c in range(n_chunks)]

    def scores(step):
        h, c = step
        kv = h // Q_GROUP
        per = LANES // HEAD_DIM
        q = qt_ref[h * HEAD_DIM:(h + 1) * HEAD_DIM, :]
        parts = [q if i == kv % per else jnp.zeros((HEAD_DIM, TQ), BF16) for i in range(per)]
        lanes = slice(kv // per * LANES, (kv // per + 1) * LANES)
        return _dot(k_ref[c * KEY_CHUNK:(c + 1) * KEY_CHUNK, lanes], jnp.concatenate(parts, axis=0))

    ones = jnp.ones((ONES_ROWS, KEY_CHUNK), BF16)
    pending = [scores(s) for s in steps[:LOOKAHEAD]]
    m_old = acc_old = None
    for i, (h, c) in enumerate(steps):
        kv = h // Q_GROUP
        st = pending.pop(0)
        if i + LOOKAHEAD < len(steps):
            pending.append(scores(steps[i + LOOKAHEAD]))
        vt = jnp.concatenate([vt_ref[kv * HEAD_DIM:(kv + 1) * HEAD_DIM, c * KEY_CHUNK:(c + 1) * KEY_CHUNK], ones],
                             axis=0)
        m = _col_reduce(jnp.max, st)
        if c == 0:
            acc = _dot(vt, jnp.exp2(st - m).astype(BF16))
        else:
            m = jnp.maximum(m_old, m)
            acc = jnp.exp2(m_old - m) * acc_old + _dot(vt, jnp.exp2(st - m).astype(BF16))
        m_old, acc_old = m, acc
        if c == n_chunks - 1:
            ot_s[h * HEAD_DIM:(h + 1) * HEAD_DIM, :] = acc[0:HEAD_DIM] / acc[HEAD_DIM:HEAD_DIM + 1]
    for c in range(HQ // LANES):
        o_ref[:, c * LANES:(c + 1) * LANES] = ot_s[c * LANES:(c + 1) * LANES, :].T.astype(o_ref.dtype)


def _attn_kernel(first_tile, qt_ref, k_ref, vt_ref, o_ref, ot_s):
    if first_tile > 0:
        _attend(qt_ref, k_ref, vt_ref, o_ref, ot_s, ROWS)
        return
    r = pl.program_id(1)

    @pl.when(r == 0)
    def _():
        _attend(qt_ref, k_ref, vt_ref, o_ref, ot_s, CTX_LEN)

    @pl.when(r != 0)
    def _():
        _attend(qt_ref, k_ref, vt_ref, o_ref, ot_s, ROWS)


def _attention(qt, k, vt, with_ctx):
    first = 0 if with_ctx else 1
    tiles = Q_TILES - first
    return pl.pallas_call(
        functools.partial(_attn_kernel, first),
        grid=(BATCH, tiles),
        in_specs=[
            pl.BlockSpec((None, HQ, TQ), lambda b, r: (b, 0, r + first)),
            pl.BlockSpec((ROWS, HKV), lambda b, r: (b, 0)),
            pl.BlockSpec((None, HKV, ROWS), lambda b, r: (b, 0, 0)),
        ],
        out_specs=pl.BlockSpec((TQ, HQ), lambda b, r: (b * tiles + r, 0)),
        out_shape=jax.ShapeDtypeStruct((BATCH * tiles * TQ, HQ), BF16),
        scratch_shapes=[pltpu.VMEM((HQ, TQ), F32)],
        compiler_params=_params("arbitrary", "arbitrary"),
        name="attention",
    )(qt, k, vt)


def _tail_kernel(lat_only, has_bias, n_x, a_ref, *refs):
    x_refs = refs[:n_x]
    refs = list(refs[n_x:])
    wm_ref = refs.pop(0)
    bm_ref = refs.pop(0) if has_bias else None
    g1_ref, cg1_ref, ng_ref, sh_ref, csh_ref, sc_ref, csc_ref, g2_ref, cg2_ref, w1_ref, w2_ref, o_ref = refs
    j = pl.program_id(1)
    is_ctx = False if lat_only else _ctx_rows(o_ref.shape[0], j)
    pick = lambda ctx_ref, ref: ref[...] if lat_only else jnp.where(is_ctx, ctx_ref[...], ref[...])
    mix = _dot(a_ref[...], wm_ref[...])
    if has_bias:
        mix = mix + bm_ref[...]
    xn = _stream_tile(x_refs, j) + pick(cg1_ref, g1_ref) * mix
    ms = jnp.mean(xn * xn, axis=-1, keepdims=True)
    h = xn * lax.rsqrt(ms + EPS) * ng_ref[...]
    h = (h * (1.0 + pick(csc_ref, sc_ref)) + pick(csh_ref, sh_ref)).astype(BF16)
    n_chunks = D_FF // TF
    up = lambda c: _dot(h, w1_ref[:, c * TF:(c + 1) * TF])
    u_next = up(0)
    acc = None
    for c in range(n_chunks):
        u = jnp.maximum(u_next, 0.0)
        if c + 1 < n_chunks:
            u_next = up(c + 1)
        part = _dot((u * u).astype(BF16), w2_ref[c * TF:(c + 1) * TF, :])
        acc = part if acc is None else acc + part
    o_ref[...] = xn + pick(cg2_ref, g2_ref) * acc


def _tail(a, x, mods, layer, w_mix, mix_idx, b_mix, norm_g, w1, w2, lat_only=False):
    if lat_only:
        tm = TM_LAT
        tiles = SEQ // tm
        row = pl.BlockSpec((pl.Element(tm), pl.Element(D_MODEL)),
                           lambda b, j: (pl.multiple_of(b * ROWS + CTX_LEN + j * tm, CTX_LEN), 0))
        x_specs, x_args = [row], [x]
        out_rows = BATCH * SEQ
        if a.shape[0] == out_rows:
            row = pl.BlockSpec((tm, D_MODEL), lambda b, j: (b * tiles + j, 0))
    else:
        tm = TM
        tiles = TILES_PER_BATCH
        row = pl.BlockSpec((tm, D_MODEL), lambda b, j: (b * tiles + j, 0))
        x_specs, x_args = _stream_specs(x)
        out_rows = BATCH * ROWS
    once = pl.Buffered(1)
    vec = pl.BlockSpec((1, D_MODEL), lambda b, j: (0, 0))
    has_bias = b_mix is not None
    return pl.pallas_call(
        functools.partial(_tail_kernel, lat_only, has_bias, len(x_args)),
        grid=(BATCH, tiles),
        in_specs=[
            row,
            *x_specs,
            _layer_spec((D_MODEL, D_MODEL), mix_idx, once),
            *([vec] if has_bias else []),
            *_mod_specs(layer, 2),
            vec,
            *_mod_specs(layer, 3), *_mod_specs(layer, 4), *_mod_specs(layer, 5),
            _layer_spec((D_MODEL, D_FF), layer, once),
            _layer_spec((D_FF, D_MODEL), layer, once),
        ],
        out_specs=pl.BlockSpec((tm, D_MODEL), lambda b, j: (b * tiles + j, 0)),
        out_shape=jax.ShapeDtypeStruct((out_rows, D_MODEL), F32),
        compiler_params=_params("arbitrary", "arbitrary"),
        name="tail",
    )(a, *x_args, w_mix, *([b_mix.reshape(1, D_MODEL)] if has_bias else []), mods, mods,
      norm_g.reshape(1, D_MODEL), mods, mods, mods, mods, mods, mods, w1, w2)


def _glu_kernel(x_ref, g_ref, sh_ref, csh_ref, sc_ref, csc_ref, w_ref, b_ref, u_ref):
    j = pl.program_id(1)
    is_ctx = _ctx_rows(TM, j)
    h = _norm_mod(x_ref[...], g_ref[...], is_ctx, sh_ref[...], csh_ref[...], sc_ref[...], csc_ref[...])
    h = h.astype(BF16)

    def proj(c):
        lo, hi = c * MXU_DIM, (c + 1) * MXU_DIM
        return (_dot(h, w_ref[:, lo:hi]) + b_ref[:, lo:hi],
                _dot(h, w_ref[:, D_MODEL + lo:D_MODEL + hi]) + b_ref[:, D_MODEL + lo:D_MODEL + hi])

    nxt = proj(0)
    for c in range(D_MODEL // MXU_DIM):
        val, gate = nxt
        if c + 1 < D_MODEL // MXU_DIM:
            nxt = proj(c + 1)
        u_ref[:, c * MXU_DIM:(c + 1) * MXU_DIM] = val * _sigmoid(gate)


def _gelu_kernel(x_ref, g_ref, sh_ref, csh_ref, sc_ref, csc_ref, w_ref, gate_ref, xr_ref):
    j = pl.program_id(1)
    is_ctx = _ctx_rows(TM, j)
    h = _norm_mod(x_ref[...], g_ref[...], is_ctx, sh_ref[...], csh_ref[...], sc_ref[...], csc_ref[...])
    h = h.astype(BF16)
    proj = lambda lo: _dot(h, w_ref[:, lo:lo + MXU_DIM])
    nxt = proj(0)
    for c in range(D_RNN // MXU_DIM):
        lo = c * MXU_DIM
        pre = nxt
        xr_ref[:, lo:lo + MXU_DIM] = proj(D_RNN + lo)
        if c + 1 < D_RNN // MXU_DIM:
            nxt = proj(lo + MXU_DIM)
        gate_ref[:, lo:lo + MXU_DIM] = jax.nn.gelu(pre)


def _in_proj(kernel, name, x, mods, layer, norm_g, w, bias, n_out):
    row = lambda b, j: (b * TILES_PER_BATCH + j, 0)
    const = lambda b, j: (0, 0)
    extra_specs = [] if bias is None else [pl.BlockSpec((1, 2 * D_MODEL), const)]
    extra_args = [] if bias is None else [bias.reshape(1, 2 * D_MODEL)]
    out_spec = pl.BlockSpec((TM, D_MODEL), row)
    out_shape = jax.ShapeDtypeStruct((BATCH * ROWS, D_MODEL), F32)
    return pl.pallas_call(
        kernel,
        grid=(BATCH, TILES_PER_BATCH),
        in_specs=[
            pl.BlockSpec((TM, D_MODEL), row),
            pl.BlockSpec((1, D_MODEL), const),
            *_mod_specs(layer, 0), *_mod_specs(layer, 1),
            pl.BlockSpec((D_MODEL, 2 * D_MODEL), const),
            *extra_specs,
        ],
        out_specs=out_spec if n_out == 1 else [out_spec] * n_out,
        out_shape=out_shape if n_out == 1 else [out_shape] * n_out,
        compiler_params=_params("arbitrary", "arbitrary"),
        name=name,
    )(x, norm_g.reshape(1, D_MODEL), mods, mods, mods, mods, w.astype(BF16), *extra_args)


CONV_ROWS = 128


def _conv_kernel(prev_ref, cur_ref, next_ref, w_ref, b_ref, ng_ref, nb_ref, a_ref, buf_s, y_s):
    r = pl.program_id(1)
    half = CONV_WIDTH // 2
    prev_ok = (r >= 2).astype(F32)
    next_ok = jnp.logical_and(r >= 1, r <= Q_TILES - 2).astype(F32)
    for c in range(D_MODEL // LANES):
        cols = slice(c * LANES, (c + 1) * LANES)
        buf_s[c, 0:HALO, :] = prev_ref[:, cols] * prev_ok
        buf_s[c, HALO:HALO + TQ, :] = cur_ref[:, cols]
        buf_s[c, HALO + TQ:HALO + TQ + HALO, :] = next_ref[:, cols] * next_ok
        w = w_ref[:, cols]

        def row_block(rb, carry):
            base = pl.multiple_of(rb * CONV_ROWS, CONV_ROWS)
            acc = jnp.zeros((CONV_ROWS, LANES), F32)
            for tap in range(CONV_WIDTH):
                rows = pl.ds(base + HALO - half + tap, CONV_ROWS, stride=1)
                acc = acc + w[tap:tap + 1, :] * buf_s[c, rows, :]
            y_s[pl.ds(base, CONV_ROWS), cols] = acc
            return carry

        lax.fori_loop(0, TQ // CONV_ROWS, row_block, 0)
    y = y_s[...] + b_ref[...]
    yc = y - jnp.mean(y, axis=-1, keepdims=True)
    var = jnp.mean(yc * yc, axis=-1, keepdims=True)
    z = yc * lax.rsqrt(var + EPS) * ng_ref[...] + nb_ref[...]
    a_ref[...] = (z * _sigmoid(z)).astype(a_ref.dtype)


def _conv_module(u, w_dw, b_dw, norm_g, norm_b):
    per = TQ // HALO
    last = BATCH * ROWS // HALO - 1
    tile = lambda b, r: b * Q_TILES + r
    const = lambda b, r: (0, 0)
    return pl.pallas_call(
        _conv_kernel,
        grid=(BATCH, Q_TILES),
        in_specs=[
            pl.BlockSpec((HALO, D_MODEL), lambda b, r: (jnp.maximum(tile(b, r) * per - 1, 0), 0)),
            pl.BlockSpec((TQ, D_MODEL), lambda b, r: (tile(b, r), 0)),
            pl.BlockSpec((HALO, D_MODEL), lambda b, r: (jnp.minimum((tile(b, r) + 1) * per, last), 0)),
            pl.BlockSpec((CONV_WIDTH, D_MODEL), const),
            pl.BlockSpec((1, D_MODEL), const),
            pl.BlockSpec((1, D_MODEL), const),
            pl.BlockSpec((1, D_MODEL), const),
        ],
        out_specs=pl.BlockSpec((TQ, D_MODEL), lambda b, r: (tile(b, r), 0)),
        out_shape=jax.ShapeDtypeStruct((BATCH * ROWS, D_MODEL), BF16),
        scratch_shapes=[
            pltpu.VMEM((D_MODEL // LANES, TQ + 2 * HALO, LANES), F32),
            pltpu.VMEM((TQ, D_MODEL), F32),
        ],
        compiler_params=_params("arbitrary", "arbitrary"),
        name="conv_module",
    )(u, u, u, w_dw, b_dw.reshape(1, D_MODEL), norm_g.reshape(1, D_MODEL), norm_b.reshape(1, D_MODEL))


def _lru_kernel(reverse, x_ref, halo_ref, cw_ref, cb_ref, wr_ref, wi_ref, br_ref, bi_ref, lam_ref, *rest):
    if reverse:
        hf_ref, gate_ref, out_ref, xp_s, a_s, b_s, hs_s, h_s = rest
    else:
        out_ref, xp_s, a_s, b_s, hs_s, h_s = rest
    i = pl.program_id(1)
    chunk = jnp.where(i == 0, 0, N_CHUNKS - i) if reverse else i
    n = BATCH * T_CHUNK
    w = LRU_CONV_WIDTH

    if reverse:
        halo_ok = jnp.logical_and(chunk >= 1, chunk <= N_CHUNKS - 2).astype(F32)
        x_at, halo_at = 0, T_CHUNK
    else:
        halo_ok = (chunk >= 2).astype(F32)
        x_at, halo_at = LRU_HALO, 0
    cw = cw_ref[...]
    cb = cb_ref[...]
    u_cols = []
    for c in range(LRU_BLOCK // LANES):
        cols = slice(c * LANES, (c + 1) * LANES)
        u_rows = []
        for bt in range(BATCH):
            base = bt * LRU_PITCH
            xp_s[c, base + x_at:base + x_at + T_CHUNK, :] = x_ref[bt, :, cols]
            xp_s[c, base + halo_at:base + halo_at + LRU_HALO, :] = halo_ref[bt, :, cols] * halo_ok
            acc = cb[:, cols]
            for s in range(w):
                start = base + x_at + (s if reverse else -s)
                acc = acc + cw[w - 1 - s:w - s, cols] * xp_s[c, pl.ds(start, T_CHUNK, stride=1), :]
            u_rows.append(acc)
        u_cols.append(jnp.concatenate(u_rows, axis=0))
    u = jnp.concatenate(u_cols, axis=1)

    ub = u.astype(BF16)
    r_gate = 0.5 * jnp.tanh(_dot(ub, wr_ref[...]) + br_ref[...]) + 0.5
    i_gate = 0.5 * jnp.tanh(_dot(ub, wi_ref[...]) + bi_ref[...]) + 0.5
    neg_rate = LRU_C * jax.nn.softplus(-lam_ref[...])
    neg_log_a = r_gate * neg_rate
    a = jnp.exp2(r_gate * (neg_rate * -LOG2E))
    sq = jnp.tanh(neg_log_a) * (a * a + 1.0)
    mult = jnp.where(sq > 0.0, sq * lax.rsqrt(sq), 0.0)
    t_idx = lax.broadcasted_iota(jnp.int32, (n, 1), 0) % T_CHUNK
    start = jnp.logical_and(t_idx == (T_CHUNK - 1 if reverse else 0), i == 0)
    mult = jnp.where(start, 1.0, mult)
    b = mult * i_gate * u
    for c in range(LRU_BLOCK // LANES):
        for bt in range(BATCH):
            rows = slice(bt * T_CHUNK, (bt + 1) * T_CHUNK)
            dst = slice(bt * LRU_PITCH, bt * LRU_PITCH + T_CHUNK)
            a_s[c, dst, :] = a[rows, c * LANES:(c + 1) * LANES]
            b_s[c, dst, :] = b[rows, c * LANES:(c + 1) * LANES]

    @pl.when(i == 0)
    def _():
        h_s[...] = jnp.zeros_like(h_s)

    def step(k, hs):
        t = T_CHUNK - 1 - k if reverse else k
        rows = pl.ds(t, BATCH, stride=LRU_PITCH)
        out = []
        for c, h in enumerate(hs):
            h = a_s[c, rows, :] * h + b_s[c, rows, :]
            hs_s[c, rows, :] = h
            out.append(h)
        return tuple(out)

    h0 = tuple(h_s[c] for c in range(LRU_BLOCK // LANES))
    h1 = lax.fori_loop(0, T_CHUNK, step, h0, unroll=8)
    for c, h in enumerate(h1):
        h_s[c] = h

    for c in range(LRU_BLOCK // LANES):
        cols = slice(c * LANES, (c + 1) * LANES)
        for bt in range(BATCH):
            hs = hs_s[c, bt * LRU_PITCH:bt * LRU_PITCH + T_CHUNK, :]
            if reverse:
                out_ref[bt, :, cols] = ((hf_ref[bt, :, cols] + hs) * gate_ref[bt, :, cols]).astype(out_ref.dtype)
            else:
                out_ref[bt, :, cols] = hs


def _lru_scan(reverse, xr, conv_w, conv_b, gate_w, gate_b, lam, hf=None, gate=None):
    per = T_CHUNK // LRU_HALO
    last = ROWS // LRU_HALO - 1
    if reverse:
        chunk = lambda i: jnp.where(i == 0, 0, N_CHUNKS - i)
        halo = lambda n, i: (0, jnp.minimum((chunk(i) + 1) * per, last), n)
    else:
        chunk = lambda i: i
        halo = lambda n, i: (0, jnp.maximum(chunk(i) * per - 1, 0), n)
    blk = pl.BlockSpec((BATCH, T_CHUNK, LRU_BLOCK), lambda n, i: (0, chunk(i), n))
    vec = pl.BlockSpec((1, LRU_BLOCK), lambda n, i: (0, n))
    mat = pl.BlockSpec((None, LRU_BLOCK, LRU_BLOCK), lambda n, i: (n, 0, 0))
    extra_specs = [blk, blk] if reverse else []
    extra_args = [hf, gate] if reverse else []
    half_w = (0.5 * gate_w).astype(BF16)
    half_b = 0.5 * gate_b
    return pl.pallas_call(
        functools.partial(_lru_kernel, reverse),
        grid=(N_LRU_BLOCKS, N_CHUNKS),
        in_specs=[
            blk,
            pl.BlockSpec((BATCH, LRU_HALO, LRU_BLOCK), halo),
            pl.BlockSpec((LRU_CONV_WIDTH, LRU_BLOCK), lambda n, i: (0, n)),
            vec, mat, mat, vec, vec, vec,
            *extra_specs,
        ],
        out_specs=blk,
        out_shape=jax.ShapeDtypeStruct((BATCH, ROWS, D_RNN), BF16 if reverse else F32),
        scratch_shapes=[
            pltpu.VMEM((LRU_BLOCK // LANES, BATCH * LRU_PITCH, LANES), F32),
            pltpu.VMEM((LRU_BLOCK // LANES, BATCH * LRU_PITCH, LANES), F32),
            pltpu.VMEM((LRU_BLOCK // LANES, BATCH * LRU_PITCH, LANES), F32),
            pltpu.VMEM((LRU_BLOCK // LANES, BATCH * LRU_PITCH, LANES), F32),
            pltpu.VMEM((LRU_BLOCK // LANES, BATCH, LANES), F32),
        ],
        compiler_params=_params("arbitrary", "arbitrary"),
        name="lru_bwd" if reverse else "lru_fwd",
    )(xr, xr, conv_w, conv_b.reshape(1, D_RNN), half_w[0], half_w[1],
      half_b[0].reshape(1, D_RNN), half_b[1].reshape(1, D_RNN), lam.reshape(1, D_RNN), *extra_args)


def kernel(x, c, ctx, c_ctx, mod_w, mod_b, norm_mix_g, norm_mlp_g, mlp_w1, mlp_w2, attn_w_qkv, attn_q_gain,
           attn_k_gain, attn_w_o, conv_w_in, conv_b_in, conv_w_dw, conv_b_dw, conv_norm_g, conv_norm_b,
           conv_w_out, conv_b_out, lru_w_in, lru_conv_w, lru_conv_b, lru_gate_w, lru_gate_b, lru_lambda,
           lru_w_out):
    cond = jnp.concatenate([c, c_ctx[None, :], jnp.zeros((COND_ROWS - BATCH - 1, D_MODEL), F32)], axis=0)
    mods = _ada_mod(cond, mod_w, mod_b)
    xs = (ctx, x)
    tables = _rope_tables()
    w_qkv, w_o = attn_w_qkv.astype(BF16), attn_w_o.astype(BF16)
    w_conv_out, w_lru_out = conv_w_out.astype(BF16), lru_w_out.astype(BF16)
    w1, w2 = mlp_w1.astype(BF16), mlp_w2.astype(BF16)
    for i in range(DEPTH):
        kind, j = i % N_MIXERS, i // N_MIXERS
        need_ctx = i < DEPTH - 1
        if kind == 0:
            qt, k, vt = _qkv_proj(xs, mods, i, norm_mix_g[i], w_qkv, j, attn_q_gain[j], attn_k_gain[j], tables)
            a = _attention(qt, k, vt, need_ctx)
            w_mix, b_mix = w_o, None
        elif kind == 1:
            u = _in_proj(_glu_kernel, "conv_in", xs, mods, i, norm_mix_g[i], conv_w_in[j], conv_b_in[j], 1)
            a = _conv_module(u, conv_w_dw[j], conv_b_dw[j], conv_norm_g[j], conv_norm_b[j])
            w_mix, b_mix = w_conv_out, conv_b_out[j]
        else:
            gate, xr = _in_proj(_gelu_kernel, "lru_in", xs, mods, i, norm_mix_g[i], lru_w_in[j], None, 2)
            gate = gate.reshape(BATCH, ROWS, D_RNN)
            xr = xr.reshape(BATCH, ROWS, D_RNN)
            hf = _lru_scan(False, xr, lru_conv_w[j, 0], lru_conv_b[j, 0], lru_gate_w[j, 0], lru_gate_b[j, 0],
                           lru_lambda[j, 0])
            a = _lru_scan(True, xr, lru_conv_w[j, 1], lru_conv_b[j, 1], lru_gate_w[j, 1], lru_gate_b[j, 1],
                          lru_lambda[j, 1], hf, gate)
            a = a.reshape(BATCH * ROWS, D_RNN)
            w_mix, b_mix = w_lru_out, None
        xs = _tail(a, xs, mods, i, w_mix, j, b_mix, norm_mlp_g[i], w1, w2, lat_only=not need_ctx)
    return xs.reshape(BATCH, SEQ, D_MODEL)
```
